```python
import math
import jax
import jax.numpy as jnp
from jax import lax
import numpy as np

D_MODEL = 4096
BATCH = 4
SEQ = 2048
DEPTH = 4
DEC_BATCH = 8
DEC_SEQ = 1
PAST_LEN = 8192
PAGE_SIZE = 128

N_MIXERS = 3
N_A = (DEPTH + 2) // 3
N_B = (DEPTH + 1) // 3
N_C = DEPTH // 3
PLE_DIM = 256
D_FF = (8 * D_MODEL + 3 * 256 - 1) // (3 * 256) * 256
LN_EPS = 1e-5
DEEPNORM_ALPHA = (2 * DEPTH) ** 0.25
DEEPNORM_BETA = (8 * DEPTH) ** -0.25

A_HEADS = 32
A_HEAD_DIM = D_MODEL // A_HEADS
A_SCALE = A_HEAD_DIM ** -0.5
MOBA_BLOCK = 256
MOBA_TOPK = 3
MOBA_Q_CHUNK = 16
REL_BUCKETS = 32
REL_MAX_DIST = 128

RET_HEADS = 16
RET_KEY_DIM = D_MODEL // RET_HEADS
RET_VAL_DIM = 2 * D_MODEL // RET_HEADS
RET_CHUNK = 128
RET_GN_EPS = 1e-5

W_HEAD = 64
W_HEADS = D_MODEL // W_HEAD
W_DECAY_LORA = 128
W_AAA_LORA = 128
W_GATE_LORA = 480
W_GN_EPS = 64e-5

NEG_INF = -1e30

kernel_name = 'moba_retnet_rwkv7_hybrid_step'


def _layer_norm(x, g, b):
    xf = x.astype(jnp.float32)
    mu = xf.mean(-1, keepdims=True)
    var = jnp.square(xf - mu).mean(-1, keepdims=True)
    return ((xf - mu) * lax.rsqrt(var + LN_EPS) * g + b).astype(x.dtype)


def _group_norm(y, eps):
    yf = y.astype(jnp.float32)
    mu = yf.mean(-1, keepdims=True)
    var = jnp.square(yf - mu).mean(-1, keepdims=True)
    return (yf - mu) * lax.rsqrt(var + eps)


def _rel_bucket(dist):
    n = jnp.maximum(dist, 0)
    max_exact = REL_BUCKETS // 2
    nf = jnp.maximum(n, 1).astype(jnp.float32)
    large = max_exact + (jnp.log(nf / max_exact) / math.log(REL_MAX_DIST / max_exact)
                         * (REL_BUCKETS - max_exact)).astype(jnp.int32)
    return jnp.where(n < max_exact, n, jnp.minimum(large, REL_BUCKETS - 1))


def _rel_bias(table_T, dist):
    H = table_T.shape[0]
    return table_T[jnp.arange(H)[:, None], _rel_bucket(dist)]


def _heads_qkv(x, w_qkv):
    B, L, _ = x.shape
    qkv = (x @ w_qkv).reshape(B, L, 3, A_HEADS, A_HEAD_DIM)
    return qkv[:, :, 0], qkv[:, :, 1], qkv[:, :, 2]


def _moba_prompt(q, k, v, rel_bias):
    B, S, H, hd = q.shape
    nb = -(-S // MOBA_BLOCK)
    n_cand = (S - 1) // MOBA_BLOCK
    k_sel_n = min(MOBA_TOPK, n_cand)
    pad = ((0, 0), (0, nb * MOBA_BLOCK - S), (0, 0), (0, 0))
    k_blk = jnp.pad(k, pad).reshape(B, nb, MOBA_BLOCK, H, hd)
    v_blk = jnp.pad(v, pad).reshape(B, nb, MOBA_BLOCK, H, hd)
    k_mean = k_blk[:, :n_cand].astype(jnp.float32).mean(axis=2)
    table_T = rel_bias.T.astype(jnp.float32)
    b_idx = jnp.arange(B)[:, None, None, None]
    h_idx = jnp.arange(H)[None, None, :, None]

    def chunk(c):
        t0 = c * MOBA_Q_CHUNK
        qc = lax.dynamic_slice_in_dim(q, t0, MOBA_Q_CHUNK, axis=1)
        q_pos = t0 + jnp.arange(MOBA_Q_CHUNK, dtype=jnp.int32)
        q_blk = t0 // MOBA_BLOCK
        k_own = lax.dynamic_index_in_dim(k_blk, q_blk, axis=1, keepdims=False)
        v_own = lax.dynamic_index_in_dim(v_blk, q_blk, axis=1, keepdims=False)
        own_pos = q_blk * MOBA_BLOCK + jnp.arange(MOBA_BLOCK, dtype=jnp.int32)
        dist = (q_pos[:, None] - own_pos[None, :])[None, :, None, :]
        s_own = (jnp.einsum('bqhd,bkhd->bqhk', qc, k_own).astype(jnp.float32) * A_SCALE
                 + _rel_bias(table_T, dist))
        s_own = jnp.where(dist >= 0, s_own, NEG_INF)
        if k_sel_n == 0:
            p = jax.nn.softmax(s_own, axis=-1)
            return jnp.einsum('bqhk,bkhd->bqhd', p, v_own.astype(jnp.float32)).astype(q.dtype)
        gate = jnp.einsum('bqhd,bnhd->bqhn', qc.astype(jnp.float32), k_mean)
        gate = jnp.where(jnp.arange(n_cand) < q_blk, gate, NEG_INF)
        _, sel = lax.top_k(gate, k_sel_n)
        n_s = k_sel_n * MOBA_BLOCK
        k_sel = k_blk[b_idx, sel, :, h_idx].reshape(B, MOBA_Q_CHUNK, H, n_s, hd)
        v_sel = v_blk[b_idx, sel, :, h_idx].reshape(B, MOBA_Q_CHUNK, H, n_s, hd)
        sel_pos = (sel[..., None] * MOBA_BLOCK + jnp.arange(MOBA_BLOCK)).reshape(B, MOBA_Q_CHUNK, H, n_s)
        s_sel = (jnp.einsum('bqhd,bqhkd->bqhk', qc, k_sel).astype(jnp.float32) * A_SCALE
                 + _rel_bias(table_T, q_pos[None, :, None, None] - sel_pos))
        s_sel = jnp.where(jnp.repeat(sel < q_blk, MOBA_BLOCK, axis=-1), s_sel, NEG_INF)
        p = jax.nn.softmax(jnp.concatenate([s_sel, s_own], axis=-1), axis=-1)
        o = (jnp.einsum('bqhk,bqhkd->bqhd', p[..., :n_s], v_sel.astype(jnp.float32))
             + jnp.einsum('bqhk,bkhd->bqhd', p[..., n_s:], v_own.astype(jnp.float32)))
        return o.astype(q.dtype)

    out = lax.map(chunk, jnp.arange(S // MOBA_Q_CHUNK, dtype=jnp.int32))
    return jnp.moveaxis(out, 0, 1).reshape(B, S, H, hd)


def _moba_sample(q, k_new, v_new, k_pool, v_pool, layer, page_table, rel_bias):
    B, T, H, hd = q.shape
    n_pages = page_table.shape[1]
    past = n_pages * PAGE_SIZE
    total = past + T
    nb_all = -(-total // MOBA_BLOCK)
    n_cand = (total - 1) // MOBA_BLOCK
    k_sel_n = min(MOBA_TOPK, n_cand)
    table_T = rel_bias.T.astype(jnp.float32)
    q_pos = past + jnp.arange(T, dtype=jnp.int32)
    q_blk = q_pos // MOBA_BLOCK
    b_idx = jnp.arange(B)[:, None, None, None]
    h_idx = jnp.arange(H)[None, None, :, None]
    own = (q_blk * MOBA_BLOCK)[:, None] + jnp.arange(MOBA_BLOCK, dtype=jnp.int32)
    pos = jnp.broadcast_to(own[None, :, None, :], (B, T, H, MOBA_BLOCK))
    valid = pos <= q_pos[None, :, None, None]
    if k_sel_n > 0:
        ppb = MOBA_BLOCK // PAGE_SIZE
        page_sums = k_pool[layer, page_table].astype(jnp.float32).sum(axis=2)
        page_sums = jnp.pad(page_sums, ((0, 0), (0, nb_all * ppb - n_pages), (0, 0), (0, 0)))
        blk_sums = page_sums.reshape(B, nb_all, ppb, H, hd).sum(axis=2)
        blk_sums = blk_sums.at[:, q_blk].add(k_new.astype(jnp.float32))
        k_mean = blk_sums[:, :n_cand] / MOBA_BLOCK
        gate = jnp.einsum('bqhd,bnhd->bqhn', q.astype(jnp.float32), k_mean)
        gate = jnp.where(jnp.arange(n_cand)[None, None, :] < q_blk[:, None, None], gate, NEG_INF)
        _, sel = lax.top_k(gate, k_sel_n)
        sel_pos = (sel[..., None] * MOBA_BLOCK + jnp.arange(MOBA_BLOCK)).reshape(B, T, H, k_sel_n * MOBA_BLOCK)
        sel_ok = jnp.repeat(sel < q_blk[:, None, None], MOBA_BLOCK, axis=-1)
        pos = jnp.concatenate([sel_pos, pos], axis=-1)
        valid = jnp.concatenate([sel_ok, valid], axis=-1)

    def fetch(pool, new):
        page = page_table[b_idx, jnp.minimum(pos // PAGE_SIZE, n_pages - 1)]
        past_rows = pool[layer, page, pos % PAGE_SIZE, h_idx]
        new_rows = new[b_idx, jnp.clip(pos - past, 0, T - 1), h_idx]
        return jnp.where((pos < past)[..., None], past_rows, new_rows)

    k_rows = fetch(k_pool, k_new)
    v_rows = fetch(v_pool, v_new)
    s = (jnp.einsum('bqhd,bqhkd->bqhk', q, k_rows).astype(jnp.float32) * A_SCALE
         + _rel_bias(table_T, q_pos[None, :, None, None] - pos))
    p = jax.nn.softmax(jnp.where(valid, s, NEG_INF), axis=-1)
    return jnp.einsum('bqhk,bqhkd->bqhd', p, v_rows.astype(jnp.float32)).astype(q.dtype)


def _rotate(x, pos):
    dk = x.shape[-1]
    inv = 1.0 / (10000.0 ** jnp.linspace(0.0, 1.0, dk // 2, dtype=jnp.float32))
    ang = pos.astype(jnp.float32)[:, None] * inv
    cos = jnp.cos(ang)[None, :, None, :]
    sin = jnp.sin(ang)[None, :, None, :]
    xf = x.astype(jnp.float32).reshape(x.shape[:-1] + (dk // 2, 2))
    x0, x1 = xf[..., 0], xf[..., 1]
    return jnp.stack([x0 * cos - x1 * sin, x1 * cos + x0 * sin], axis=-1).reshape(x.shape)


def _retention(q, k, v, s0):
    B, L, H, dk = q.shape
    dv = v.shape[-1]
    C = RET_CHUNK if L % RET_CHUNK == 0 else L
    n = L // C
    log_g = jnp.log1p(-jnp.exp2(-5.0 - jnp.arange(H, dtype=jnp.float32)))
    idx = jnp.arange(C, dtype=jnp.float32)
    diff = idx[:, None] - idx[None, :]
    dmask = jnp.where(diff[None] >= 0, jnp.exp(jnp.maximum(diff, 0.0)[None] * log_g[:, None, None]), 0.0)
    q_dec = jnp.exp((idx[None, :] + 1.0) * log_g[:, None]).T[None, :, :, None]
    k_dec = jnp.exp((C - 1.0 - idx[None, :]) * log_g[:, None]).T[None, :, :, None]
    c_dec = jnp.exp(C * log_g)[None, :, None, None]

    def step(S, inp):
        qc, kc, vc = inp
        qk = jnp.einsum('bihd,bjhd->bhij', qc, kc) * dmask[None]
        o = jnp.einsum('bhij,bjhe->bihe', qk, vc) + jnp.einsum('bihd,bhde->bihe', qc * q_dec, S)
        S = S * c_dec + jnp.einsum('bjhd,bjhe->bhde', kc * k_dec, vc)
        return S, o

    to_chunks = lambda t: jnp.moveaxis(t.reshape(B, n, C, H, t.shape[-1]), 1, 0)
    s_new, o = lax.scan(step, s0, (to_chunks(q), to_chunks(k), to_chunks(v)))
    return jnp.moveaxis(o, 0, 1).reshape(B, L, H, dv), s_new


def _retnet_mixer(x, pos0, s0, w_in, gn_g, w_o):
    B, L, D = x.shape
    q, k, v, g = jnp.split(x @ w_in, [D, 2 * D, 4 * D], axis=-1)
    pos = pos0 + jnp.arange(L, dtype=jnp.int32)
    q = _rotate(q.reshape(B, L, RET_HEADS, RET_KEY_DIM), pos)
    k = _rotate(k.reshape(B, L, RET_HEADS, RET_KEY_DIM), pos) * RET_KEY_DIM ** -0.5
    v = v.reshape(B, L, RET_HEADS, RET_VAL_DIM).astype(jnp.float32)
    y, s_new = _retention(q, k, v, s0.astype(jnp.float32))
    y = _group_norm(y, RET_GN_EPS).reshape(B, L, 2 * D) * gn_g
    out = (jax.nn.silu(g.astype(jnp.float32)) * y).astype(x.dtype) @ w_o
    return out, s_new.astype(s0.dtype)


def _rwkv7_mixer(x, shift0, s0, mu, w_rkv, w0, w1, w2, a0, a1, a2, g1, g2, k_k, k_a, r_k, gn_g, gn_b, w_o):
    B, L, D = x.shape
    H, N = W_HEADS, W_HEAD
    f32 = jnp.float32
    x_prev = jnp.concatenate([shift0[:, None].astype(x.dtype), x[:, :-1]], axis=1)
    xx = x_prev - x
    xm = x[None] + xx[None] * mu[:, None, None, :]
    r, k, v = jnp.einsum('ibld,ide->ible', xm[:3], w_rkv)
    xw, xa, xg = xm[3], xm[4], xm[5]
    w_log = -jax.nn.softplus(-(w0 + jnp.tanh(xw @ w1) @ w2).astype(f32)) - 0.5
    decay = jnp.exp(-jnp.exp(w_log))
    a = jax.nn.sigmoid((a0 + (xa @ a1) @ a2).astype(f32)).reshape(B, L, H, N)
    g = jax.nn.sigmoid(xg @ g1) @ g2
    kk = (k * k_k).astype(f32).reshape(B, L, H, N)
    kk = kk / jnp.maximum(jnp.sqrt(jnp.sum(kk * kk, axis=-1, keepdims=True)), 1e-12)
    k_h = k.astype(f32).reshape(B, L, H, N) * (1.0 + (a - 1.0) * k_a.reshape(H, N))
    r_h = r.astype(f32).reshape(B, L, H, N)
    v_h = v.astype(f32).reshape(B, L, H, N)
    tm = lambda t: jnp.moveaxis(t.reshape(B, L, H, N).astype(f32), 1, 0)

    def step(S, inp):
        r_t, w_t, k_t, v_t, ka_t, kb_t = inp
        S = (S * w_t[:, :, None, :]
             + jnp.einsum('bhvk,bhk->bhv', S, ka_t)[..., None] * kb_t[:, :, None, :]
             + v_t[..., None] * k_t[:, :, None, :])
        return S, jnp.einsum('bhvk,bhk->bhv', S, r_t)

    s_new, y = lax.scan(step, s0.astype(f32), (tm(r_h), tm(decay), tm(k_h), tm(v_h), tm(-kk), tm(kk * a)))
    y = jnp.moveaxis(y, 0, 1)
    bonus = jnp.sum(r_h * k_h * r_k, axis=-1, keepdims=True) * v_h
    y = _group_norm(y, W_GN_EPS).reshape(B, L, D) * gn_g + gn_b + bonus.reshape(B, L, D)
    out = (y * g.astype(f32)).astype(x.dtype) @ w_o
    return out, x[:, -1], s_new.astype(s0.dtype)


def _layer_tail(x, mix, p, ln1_g, ln1_b, ln2_g, ln2_b, w_ffn_in, w_ffn_out, w_ple_gate, w_ple_proj):
    h = _layer_norm(DEEPNORM_ALPHA * x + mix, ln1_g, ln1_b)
    gate, up = jnp.split(h @ w_ffn_in, 2, axis=-1)
    ffn = (jax.nn.silu(gate) * up) @ w_ffn_out
    ple = jax.nn.sigmoid(h @ w_ple_gate) * (p.astype(h.dtype) @ w_ple_proj)
    return _layer_norm(DEEPNORM_ALPHA * h + ffn + ple, ln2_g, ln2_b)


def setup_inputs(seed: int = 0) -> dict:
    key = jax.random.key(seed)
    keys = iter(jax.random.split(key, 64))
    f32 = jnp.float32

    def nrm(shape, scale=1.0):
        return jax.random.normal(next(keys), shape, f32) * scale

    D, F = D_MODEL, D_FF
    beta = DEEPNORM_BETA
    n_pages = PAST_LEN // PAGE_SIZE
    n_used = DEC_BATCH * n_pages
    n_pool = n_used + max(1, n_used // 4)
    return {
        'x_prompt': nrm((BATCH, SEQ, D)),
        'x_sample': nrm((DEC_BATCH, DEC_SEQ, D)),
        'p_prompt': nrm((DEPTH, BATCH, SEQ, PLE_DIM)),
        'p_sample': nrm((DEPTH, DEC_BATCH, DEC_SEQ, PLE_DIM)),
        'cache_moba_k': nrm((N_A, n_pool, PAGE_SIZE, A_HEADS, A_HEAD_DIM)),
        'cache_moba_v': nrm((N_A, n_pool, PAGE_SIZE, A_HEADS, A_HEAD_DIM)),
        'page_table': jax.random.permutation(next(keys), n_pool)[:n_used].reshape(DEC_BATCH, n_pages).astype(jnp.int32),
        'state_ret': nrm((N_B, DEC_BATCH, RET_HEADS, RET_KEY_DIM, RET_VAL_DIM), 0.1),
        'state_wkv': nrm((N_C, DEC_BATCH, W_HEADS, W_HEAD, W_HEAD), 0.3),
        'state_shift': nrm((N_C, DEC_BATCH, D)),
        'ln1_g': 1.0 + nrm((DEPTH, D), 0.02),
        'ln1_b': nrm((DEPTH, D), 0.02),
        'ln2_g': 1.0 + nrm((DEPTH, D), 0.02),
        'ln2_b': nrm((DEPTH, D), 0.02),
        'w_ffn_in': nrm((DEPTH, D, 2 * F), D ** -0.5),
        'w_ffn_out': nrm((DEPTH, F, D), beta * F ** -0.5),
        'w_ple_gate': nrm((DEPTH, D, D), D ** -0.5),
        'w_ple_proj': nrm((DEPTH, PLE_DIM, D), beta * PLE_DIM ** -0.5),
        'rel_bias': nrm((REL_BUCKETS, A_HEADS), 0.5),
        'a_w_qkv': nrm((N_A, D, 3 * D), D ** -0.5),
        'a_w_o': nrm((N_A, D, D), beta * D ** -0.5),
        'r_w_in': nrm((N_B, D, 6 * D), D ** -0.5),
        'r_gn_g': 1.0 + nrm((N_B, 2 * D), 0.02),
        'r_w_o': nrm((N_B, 2 * D, D), beta * (2 * D) ** -0.5),
        'c_mu': jax.random.uniform(next(keys), (N_C, 6, D), f32),
        'c_w_rkv': nrm((N_C, 3, D, D), D ** -0.5),
        'c_w0': -1.0 + nrm((N_C, D), 0.5),
        'c_w1': nrm((N_C, D, W_DECAY_LORA), D ** -0.5),
        'c_w2': nrm((N_C, W_DECAY_LORA, D), 0.1 * W_DECAY_LORA ** -0.5),
        'c_a0': nrm((N_C, D), 0.1),
        'c_a1': nrm((N_C, D, W_AAA_LORA), D ** -0.5),
        'c_a2': nrm((N_C, W_AAA_LORA, D), 0.1 * W_AAA_LORA ** -0.5),
        'c_g1': nrm((N_C, D, W_GATE_LORA), D ** -0.5),
        'c_g2': nrm((N_C, W_GATE_LORA, D), W_GATE_LORA ** -0.5),
        'c_k_k': 0.85 + nrm((N_C, D), 0.05),
        'c_k_a': 1.0 + nrm((N_C, D), 0.05),
        'c_r_k': nrm((N_C, W_HEADS, W_HEAD), 0.1),
        'c_gn_g': 1.0 + nrm((N_C, D), 0.02),
        'c_gn_b': nrm((N_C, D), 0.02),
        'c_w_o': nrm((N_C, D, D), beta * D ** -0.5),
    }


def reference(x_prompt, x_sample, p_prompt, p_sample, cache_moba_k, cache_moba_v, page_table,
              state_ret, state_wkv, state_shift,
              ln1_g, ln1_b, ln2_g, ln2_b, w_ffn_in, w_ffn_out, w_ple_gate, w_ple_proj, rel_bias,
              a_w_qkv, a_w_o, r_w_in, r_gn_g, r_w_o,
              c_mu, c_w_rkv, c_w0, c_w1, c_w2, c_a0, c_a1, c_a2, c_g1, c_g2,
              c_k_k, c_k_a, c_r_k, c_gn_g, c_gn_b, c_w_o):
    past = page_table.shape[1] * PAGE_SIZE
    xp, xs = x_prompt, x_sample
    Bp, Sp, D = xp.shape
    Bs, Ts, _ = xs.shape
    kp_l, vp_l, ks_l, vs_l = [], [], [], []
    rp_l, rs_l = [], []
    wp_l, ws_l, hp_l, hs_l = [], [], [], []
    for i in range(DEPTH):
        kind, j = i % N_MIXERS, i // N_MIXERS
        if kind == 0:
            qp, kp, vp = _heads_qkv(xp, a_w_qkv[j])
            qs, ks, vs = _heads_qkv(xs, a_w_qkv[j])
            mp = _moba_prompt(qp, kp, vp, rel_bias).reshape(Bp, Sp, D) @ a_w_o[j]
            ms = _moba_sample(qs, ks, vs, cache_moba_k, cache_moba_v, j, page_table, rel_bias).reshape(Bs, Ts, D) @ a_w_o[j]
            kp_l.append(kp)
            vp_l.append(vp)
            ks_l.append(ks)
            vs_l.append(vs)
        elif kind == 1:
            s0 = jnp.zeros((Bp, RET_HEADS, RET_KEY_DIM, RET_VAL_DIM), state_ret.dtype)
            mp, sp = _retnet_mixer(xp, 0, s0, r_w_in[j], r_gn_g[j], r_w_o[j])
            ms, ss = _retnet_mixer(xs, past, state_ret[j], r_w_in[j], r_gn_g[j], r_w_o[j])
            rp_l.append(sp)
            rs_l.append(ss)
        else:
            prm = (c_mu[j], c_w_rkv[j], c_w0[j], c_w1[j], c_w2[j], c_a0[j], c_a1[j], c_a2[j], c_g1[j], c_g2[j],
                   c_k_k[j], c_k_a[j], c_r_k[j], c_gn_g[j], c_gn_b[j], c_w_o[j])
            sh0 = jnp.zeros((Bp, D), state_shift.dtype)
            wk0 = jnp.zeros((Bp, W_HEADS, W_HEAD, W_HEAD), state_wkv.dtype)
            mp, shp, wkp = _rwkv7_mixer(xp, sh0, wk0, *prm)
            ms, shs, wks = _rwkv7_mixer(xs, state_shift[j], state_wkv[j], *prm)
            wp_l.append(wkp)
            ws_l.append(wks)
            hp_l.append(shp)
            hs_l.append(shs)
        tail = (ln1_g[i], ln1_b[i], ln2_g[i], ln2_b[i], w_ffn_in[i], w_ffn_out[i], w_ple_gate[i], w_ple_proj[i])
        xp = _layer_tail(xp, mp, p_prompt[i], *tail)
        xs = _layer_tail(xs, ms, p_sample[i], *tail)
    return (xp, xs,
            jnp.stack(kp_l), jnp.stack(vp_l), jnp.stack(ks_l), jnp.stack(vs_l),
            jnp.stack(rp_l), jnp.stack(rs_l),
            jnp.stack(wp_l), jnp.stack(ws_l),
            jnp.stack(hp_l), jnp.stack(hs_l))
```

```python
import functools
import math

import numpy as np
import jax
import jax.numpy as jnp
from jax import lax
from jax.experimental import pallas as pl
from jax.experimental.pallas import tpu as pltpu

F32 = jnp.float32
BF16 = jnp.bfloat16

DEPTH = 4
LN_EPS = 1e-5
ALPHA = (2 * DEPTH) ** 0.25
A_HEADS = 32
MOBA_BLOCK = 256
MOBA_TOPK = 3
PAGE_SIZE = 128
REL_BUCKETS = 32
REL_MAX_DIST = 128
RET_HEADS = 16
RET_CHUNK = 128
RET_GN_EPS = 1e-5
W_HEAD = 64
W_GN_EPS = 64e-5
W_GATE_LORA_PAD = 512
NEG_INF = -1e30

V7X_VMEM_LIMIT_BYTES = 56 * 1024 * 1024
LANES = 128
BF16_SUBLANES = 16

RWKV_CHUNK = 64
RWKV_HEADS_PER_STEP = 8
SAMPLE_ROWS = 16


def _t5_bucket_thresholds():
    n = np.arange(0, 4 * REL_MAX_DIST)
    max_exact = REL_BUCKETS // 2
    nf = np.maximum(n, 1).astype(np.float32)
    large = max_exact + (np.log(nf / np.float32(max_exact)) / np.float32(math.log(REL_MAX_DIST / max_exact))
                         * np.float32(REL_BUCKETS - max_exact)).astype(np.int32)
    bucket = np.where(n < max_exact, n, np.minimum(large, REL_BUCKETS - 1))
    assert sorted(set(bucket.tolist())) == list(range(REL_BUCKETS))
    return tuple(int(np.argmax(bucket >= b)) for b in range(REL_BUCKETS))


T5_THRESHOLDS = _t5_bucket_thresholds()


def _params(*sem):
    return pltpu.CompilerParams(dimension_semantics=sem, vmem_limit_bytes=V7X_VMEM_LIMIT_BYTES)


def _bias_from_distance(dist, table_ref, h):
    t = jnp.full(dist.shape, table_ref[0, h], F32)
    for b in range(1, REL_BUCKETS):
        t = jnp.where(dist >= T5_THRESHOLDS[b], table_ref[b, h], t)
    return t


def _mm_kernel(epi, n_w, n_ex, x_ref, *refs):
    w_refs = refs[:n_w]
    ex_refs = refs[n_w:n_w + n_ex]
    o_refs = refs[n_w + n_ex:]
    x = x_ref[...]
    accs = [jnp.dot(x, w[...], preferred_element_type=F32) for w in w_refs]
    outs = epi(*accs, *[e[...] for e in ex_refs])
    if not isinstance(outs, tuple):
        outs = (outs,)
    for o_ref, o in zip(o_refs, outs):
        o_ref[...] = o.astype(o_ref.dtype)


def matmul(x, ws, *, n, tm, tn, epi=None, extras=(), out_dtypes=(F32,), name="mm"):
    M, K = x.shape
    assert M % tm == 0 and n % tn == 0, (M, tm, n, tn)
    if epi is None:
        epi = lambda a: a
    in_specs = [pl.BlockSpec((tm, K), lambda i, j: (i, 0))]
    args = [x]
    for w, col0 in ws:
        assert w.shape[0] == K and col0 % tn == 0
        in_specs.append(pl.BlockSpec((K, tn), lambda i, j, cb=col0 // tn: (0, j + cb)))
        args.append(w)
    for e in extras:
        if isinstance(e, tuple):
            row, col0 = e
            assert row.shape[0] == 1 and col0 % tn == 0
            in_specs.append(pl.BlockSpec((1, tn), lambda i, j, cb=col0 // tn: (0, j + cb)))
            args.append(row)
        else:
            assert e.shape == (M, n)
            in_specs.append(pl.BlockSpec((tm, tn), lambda i, j: (i, j)))
            args.append(e)
    out_shape = tuple(jax.ShapeDtypeStruct((M, n), dt) for dt in out_dtypes)
    out_specs = tuple(pl.BlockSpec((tm, tn), lambda i, j: (i, j)) for _ in out_dtypes)
    res = pl.pallas_call(
        functools.partial(_mm_kernel, epi, len(ws), len(extras)),
        grid=(M // tm, n // tn),
        in_specs=in_specs,
        out_specs=out_specs,
        out_shape=out_shape,
        compiler_params=_params("arbitrary", "arbitrary"),
        name=name,
    )(*args)
    return res if len(out_dtypes) > 1 else res[0]


def _ln_kernel(x_ref, g_ref, b_ref, o32_ref, o16_ref):
    x = x_ref[...]
    mu = jnp.mean(x, axis=-1, keepdims=True)
    xc = x - mu
    var = jnp.mean(xc * xc, axis=-1, keepdims=True)
    y = xc * lax.rsqrt(var + LN_EPS) * g_ref[...] + b_ref[...]
    o32_ref[...] = y
    o16_ref[...] = y.astype(BF16)


def layer_norm(x, g, b, *, tm):
    M, D = x.shape
    row = pl.BlockSpec((1, D), lambda i: (0, 0))
    tile = pl.BlockSpec((tm, D), lambda i: (i, 0))
    return pl.pallas_call(
        _ln_kernel,
        grid=(M // tm,),
        in_specs=[tile, row, row],
        out_specs=(tile, tile),
        out_shape=(jax.ShapeDtypeStruct((M, D), F32), jax.ShapeDtypeStruct((M, D), BF16)),
        compiler_params=_params("arbitrary"),
        name="layer_norm",
    )(x, g.reshape(1, D), b.reshape(1, D))


def _moba_prompt_kernel(nb, tab_ref, q_ref, k_ref, v_ref, o_ref, bown, bprev, kmean, m_s, l_s, acc_s):
    h = pl.program_id(0)
    b = pl.program_id(1)
    qb = pl.program_id(2)
    blk = MOBA_BLOCK
    n_cand = nb - 1
    k_sel_n = min(MOBA_TOPK, n_cand)
    hd = q_ref.shape[1]
    a_scale = hd ** -0.5
    row = lax.broadcasted_iota(jnp.int32, (blk, blk), 0)
    col = lax.broadcasted_iota(jnp.int32, (blk, blk), 1)

    @pl.when((b == 0) & (qb == 0))
    def _():
        bown[...] = _bias_from_distance(jnp.maximum(row - col, 0), tab_ref, h)
        bprev[...] = _bias_from_distance(row - col + blk, tab_ref, h)

    @pl.when(qb == 0)
    def _():
        kmean[...] = jnp.zeros(kmean.shape, F32)
        for n in range(n_cand):
            kmean[n:n + 1, :] = jnp.mean(k_ref[n * blk:(n + 1) * blk, :], axis=0, keepdims=True)

    q = q_ref[...]
    q16 = q.astype(BF16)
    dn_t = (((1,), (1,)), ((), ()))

    start = pl.multiple_of(qb * blk, blk)
    k_own = k_ref[pl.ds(start, blk), :].astype(BF16)
    v_own = v_ref[pl.ds(start, blk), :].astype(BF16)
    s = lax.dot_general(q16, k_own, dn_t, preferred_element_type=F32) * a_scale + bown[...]
    s = jnp.where(row >= col, s, NEG_INF)
    m0 = jnp.max(s, axis=1, keepdims=True)
    p = jnp.exp(s - m0)
    m_s[...] = m0
    l_s[...] = jnp.sum(p, axis=1, keepdims=True)
    acc_s[...] = jnp.dot(p.astype(BF16), v_own, preferred_element_type=F32)

    if n_cand > 0:
        ncols = kmean.shape[0]
        gate = lax.dot_general(q, kmean[...], dn_t, precision=lax.Precision.HIGHEST,
                               preferred_element_type=F32)
        ncol = lax.broadcasted_iota(jnp.int32, (blk, ncols), 1)
        gate = jnp.where(ncol < qb, gate, NEG_INF)
        rank = jnp.zeros((blk, ncols), F32)
        for m in range(n_cand):
            gm = gate[:, m:m + 1]
            ahead = (gm > gate) | ((gm == gate) & (ncol > m))
            rank = rank + jnp.where(ahead, 1.0, 0.0)
        sel = jnp.where((rank < k_sel_n) & (ncol < qb), 1.0, 0.0)
        far_bias = tab_ref[REL_BUCKETS - 1, h]

        for n in range(n_cand):
            @pl.when(n < qb)
            def _(n=n):
                kn = k_ref[n * blk:(n + 1) * blk, :].astype(BF16)
                vn = v_ref[n * blk:(n + 1) * blk, :].astype(BF16)
                bias = jnp.where(n == qb - 1, bprev[...], far_bias)
                sn = lax.dot_general(q16, kn, dn_t, preferred_element_type=F32) * a_scale + bias
                sn = jnp.where(sel[:, n:n + 1] > 0.5, sn, NEG_INF)
                m_old = m_s[...]
                m_new = jnp.maximum(m_old, jnp.max(sn, axis=1, keepdims=True))
                scale_old = jnp.exp(m_old - m_new)
                pn = jnp.exp(sn - m_new)
                l_s[...] = scale_old * l_s[...] + jnp.sum(pn, axis=1, keepdims=True)
                acc_s[...] = scale_old * acc_s[...] + jnp.dot(pn.astype(BF16), vn, preferred_element_type=F32)
                m_s[...] = m_new

    o_ref[...] = (acc_s[...] / l_s[...]).astype(o_ref.dtype)


def moba_prompt(q, k, v, rel_bias, *, B, S, H):
    M, D = q.shape
    hd = D // H
    blk = MOBA_BLOCK
    assert S % blk == 0 and M == B * S and T5_THRESHOLDS[-1] <= blk
    nb = S // blk
    ncols = max(8, -(-(nb - 1) // 8) * 8)
    qspec = pl.BlockSpec((blk, hd), lambda h, b, qb: (b * nb + qb, h))
    kvspec = pl.BlockSpec((S, hd), lambda h, b, qb: (b, h))
    return pl.pallas_call(
        functools.partial(_moba_prompt_kernel, nb),
        grid=(H, B, nb),
        in_specs=[pl.BlockSpec(memory_space=pltpu.SMEM), qspec, kvspec, kvspec],
        out_specs=qspec,
        out_shape=jax.ShapeDtypeStruct((M, D), BF16),
        scratch_shapes=[pltpu.VMEM((blk, blk), F32), pltpu.VMEM((blk, blk), F32), pltpu.VMEM((ncols, hd), F32),
                        pltpu.VMEM((blk, 1), F32), pltpu.VMEM((blk, 1), F32), pltpu.VMEM((blk, hd), F32)],
        compiler_params=_params("arbitrary", "arbitrary", "arbitrary"),
        name="moba_prompt",
    )(rel_bias, q, k, v)


def _moba_select_kernel(n_blocks, n_heads, pt_ref, pa_ref, pb_ref, q_ref, e_ref, sel_ref, sums):
    j = pl.program_id(1)
    page_sum = (jnp.sum(pa_ref[0, 0], axis=0, keepdims=True)
                + jnp.sum(pb_ref[0, 0], axis=0, keepdims=True))
    sums[pl.ds(j, 1), :] = page_sum

    @pl.when(j == n_blocks - 1)
    def _():
        prod = (sums[...] * (1.0 / MOBA_BLOCK)) * q_ref[0]
        p_hi = prod.astype(BF16)
        r1 = prod - p_hi.astype(F32)
        p_mid = r1.astype(BF16)
        p_lo = (r1 - p_mid.astype(F32)).astype(BF16)
        e = e_ref[...]
        gate = (jnp.dot(p_hi, e, preferred_element_type=F32) + jnp.dot(p_mid, e, preferred_element_type=F32)
                + jnp.dot(p_lo, e, preferred_element_type=F32))
        blk_id = lax.broadcasted_iota(jnp.int32, gate.shape, 0)
        rank = jnp.zeros(gate.shape, F32)
        for m in range(n_blocks):
            gm = gate[m:m + 1, :]
            ahead = (gm > gate) | ((gm == gate) & (blk_id > m))
            rank = rank + jnp.where(ahead, 1.0, 0.0)
        for r in range(sel_ref.shape[1]):
            sel_ref[0, r:r + 1, :] = jnp.sum(jnp.where(rank == float(r), blk_id, 0), axis=0, keepdims=True)


def _moba_sample_attn_kernel(n_sel_pages, past, pt_ref, sel_ref, tab_ref, q_ref, kn_ref, vn_ref, *refs):
    k_refs = refs[:n_sel_pages]
    v_refs = refs[n_sel_pages:2 * n_sel_pages]
    o_ref = refs[2 * n_sel_pages]
    b = pl.program_id(0)
    h = pl.program_id(1)
    n_heads = pl.num_programs(1)
    hd = q_ref.shape[-1]
    a_scale = hd ** -0.5
    pages_per_block = MOBA_BLOCK // PAGE_SIZE
    q = q_ref[0]
    s_own = jnp.sum(q * kn_ref[0], axis=-1, keepdims=True) * a_scale + tab_ref[0, h]
    r = lax.broadcasted_iota(jnp.int32, (PAGE_SIZE, 1), 0)
    scores = []
    for t in range(n_sel_pages):
        blk = sel_ref[(b * n_heads + h) * MOBA_TOPK + t // pages_per_block]
        pos0 = blk * MOBA_BLOCK + (t % pages_per_block) * PAGE_SIZE
        dist = past - (pos0 + r)
        bias = _bias_from_distance(dist, tab_ref, h)
        scores.append(jnp.sum(k_refs[t][0, 0] * q, axis=-1, keepdims=True) * a_scale + bias)
    m = s_own
    for s in scores:
        m = jnp.maximum(m, jnp.max(s, axis=0, keepdims=True))
    p_own = jnp.exp(s_own - m)
    l = p_own
    acc = p_own * vn_ref[0]
    for t, s in enumerate(scores):
        p = jnp.exp(s - m)
        l = l + jnp.sum(p, axis=0, keepdims=True)
        acc = acc + jnp.sum(p * v_refs[t][0, 0], axis=0, keepdims=True)
    o_ref[0] = (acc / l).astype(o_ref.dtype)


def moba_sample(q, k_new, v_new, k_pool, v_pool, layer, page_table, rel_bias, *, H):
    rows, D = q.shape
    B, n_pages = page_table.shape
    assert rows == B
    hd = D // H
    past = n_pages * PAGE_SIZE
    ppb = MOBA_BLOCK // PAGE_SIZE
    assert past % MOBA_BLOCK == 0 and H <= LANES
    n_blocks = past // MOBA_BLOCK
    assert n_blocks >= MOBA_TOPK
    pt = page_table.reshape(-1)
    q3 = q.reshape(rows, 1, D)
    e = (jnp.arange(D)[:, None] // hd == jnp.arange(LANES)[None, :]).astype(BF16)

    page = lambda half: pl.BlockSpec(
        (1, 1, PAGE_SIZE, D), lambda b, j, pt_ref: (layer, pt_ref[b * n_pages + ppb * j + half], 0, 0))
    sel_rows = 8
    sel = pl.pallas_call(
        functools.partial(_moba_select_kernel, n_blocks, H),
        grid_spec=pltpu.PrefetchScalarGridSpec(
            num_scalar_prefetch=1,
            grid=(B, n_blocks),
            in_specs=[page(0), page(1),
                      pl.BlockSpec((1, 1, D), lambda b, j, pt_ref: (b, 0, 0)),
                      pl.BlockSpec((D, LANES), lambda b, j, pt_ref: (0, 0))],
            out_specs=pl.BlockSpec((1, sel_rows, LANES), lambda b, j, pt_ref: (b, 0, 0)),
            scratch_shapes=[pltpu.VMEM((n_blocks, D), F32)]),
        out_shape=jax.ShapeDtypeStruct((B, sel_rows, LANES), jnp.int32),
        compiler_params=_params("arbitrary", "arbitrary"),
        name="moba_sample_select",
    )(pt, k_pool, k_pool, q3, e)
    sel_flat = jnp.transpose(sel[:, :MOBA_TOPK, :H], (0, 2, 1)).reshape(-1)

    n_sel_pages = MOBA_TOPK * ppb

    def sel_page(t):
        return pl.BlockSpec(
            (1, 1, PAGE_SIZE, hd),
            lambda b, h, pt_ref, sel_ref: (
                layer, pt_ref[b * n_pages + ppb * sel_ref[(b * H + h) * MOBA_TOPK + t // ppb] + t % ppb], 0, h))

    tok = pl.BlockSpec((1, 1, hd), lambda b, h, pt_ref, sel_ref: (b, 0, h))
    out = pl.pallas_call(
        functools.partial(_moba_sample_attn_kernel, n_sel_pages, past),
        grid_spec=pltpu.PrefetchScalarGridSpec(
            num_scalar_prefetch=2,
            grid=(B, H),
            in_specs=[pl.BlockSpec(memory_space=pltpu.SMEM), tok, tok, tok]
                     + [sel_page(t) for t in range(n_sel_pages)] * 2,
            out_specs=tok),
        out_shape=jax.ShapeDtypeStruct((rows, 1, D), BF16),
        compiler_params=_params("arbitrary", "arbitrary"),
        name="moba_sample_attn",
    )(pt, sel_flat, rel_bias, q3, k_new.reshape(rows, 1, D), v_new.reshape(rows, 1, D),
      *([k_pool] * n_sel_pages), *([v_pool] * n_sel_pages))
    return out.reshape(rows, D)


def _retention_kernel(c_real, lg_ref, q_ref, k_ref, v_ref, g_ref, cos_ref, sin_ref, gn_ref, s0_ref,
                      y_ref, s_out_ref, s_acc):
    h = pl.program_id(1)
    c = pl.program_id(2)
    n_chunks = pl.num_programs(2)
    C, dk = q_ref.shape

    @pl.when(c == 0)
    def _():
        s_acc[...] = s0_ref[0, 0]

    lg = lg_ref[h]
    cos = cos_ref[...]
    sin = sin_ref[...]
    even = (lax.broadcasted_iota(jnp.int32, (C, LANES), 1) % 2) == 0

    def rotate(x):
        tiles = []
        for t in range(dk // LANES):
            xt = x[:, t * LANES:(t + 1) * LANES]
            tiles.append(jnp.where(even, pltpu.roll(xt, LANES - 1, 1), pltpu.roll(xt, 1, 1)))
        return x * cos + jnp.concatenate(tiles, axis=1) * sin

    q = rotate(q_ref[...])
    k = rotate(k_ref[...]) * (dk ** -0.5)
    v16 = v_ref[...].astype(BF16)
    idx = lax.broadcasted_iota(jnp.int32, (C, 1), 0).astype(F32)
    q_dec = jnp.exp((idx + 1.0) * lg)
    k_dec = jnp.exp((c_real - 1.0 - idx) * lg)
    ii = lax.broadcasted_iota(jnp.int32, (C, C), 0)
    jj = lax.broadcasted_iota(jnp.int32, (C, C), 1)
    diff = (ii - jj).astype(F32)
    dmask = jnp.where(diff >= 0, jnp.exp(jnp.maximum(diff, 0.0) * lg), 0.0)
    q16 = q.astype(BF16)
    qk = lax.dot_general(q16, k.astype(BF16), (((1,), (1,)), ((), ())), preferred_element_type=F32) * dmask
    s_prev = s_acc[...]
    o = (jnp.dot(qk.astype(BF16), v16, preferred_element_type=F32)
         + jnp.dot((q * q_dec).astype(BF16), s_prev.astype(BF16), preferred_element_type=F32))
    kv = lax.dot_general((k * k_dec).astype(BF16), v16, (((0,), (0,)), ((), ())), preferred_element_type=F32)
    c_dec = jnp.exp(jnp.full((1, 1), c_real, F32) * lg)
    s_new = s_prev * c_dec + kv
    s_acc[...] = s_new

    mu = jnp.mean(o, axis=-1, keepdims=True)
    oc = o - mu
    var = jnp.mean(oc * oc, axis=-1, keepdims=True)
    g = g_ref[...]
    y = oc * lax.rsqrt(var + RET_GN_EPS) * gn_ref[...]
    y_ref[...] = (g * jax.nn.sigmoid(g) * y).astype(y_ref.dtype)

    @pl.when(c == n_chunks - 1)
    def _():
        s_out_ref[0, 0] = s_new


def retention(qkvg, s0, gn_g, cos_t, sin_t, *, B, L, H, C, c_real):
    M, N6 = qkvg.shape
    D = N6 // 6
    dk, dv = D // H, 2 * D // H
    assert L % C == 0 and M == B * L
    nC = L // C
    log_g = jnp.log1p(-jnp.exp2(-5.0 - jnp.arange(H, dtype=F32)))
    tokspec = lambda width, blk0: pl.BlockSpec((C, width), lambda b, h, c: (b * nC + c, blk0 + h))
    tabspec = pl.BlockSpec((C, dk), lambda b, h, c: (c, 0))
    sspec = pl.BlockSpec((1, 1, dk, dv), lambda b, h, c: (b, h, 0, 0))
    return pl.pallas_call(
        functools.partial(_retention_kernel, float(c_real)),
        grid=(B, H, nC),
        in_specs=[pl.BlockSpec(memory_space=pltpu.SMEM),
                  tokspec(dk, 0), tokspec(dk, D // dk), tokspec(dv, 2 * D // dv), tokspec(dv, 4 * D // dv),
                  tabspec, tabspec, pl.BlockSpec((1, dv), lambda b, h, c: (0, h)), sspec],
        out_specs=(pl.BlockSpec((C, dv), lambda b, h, c: (b * nC + c, h)), sspec),
        out_shape=(jax.ShapeDtypeStruct((M, 2 * D), BF16), jax.ShapeDtypeStruct(s0.shape, F32)),
        scratch_shapes=[pltpu.VMEM((dk, dv), F32)],
        compiler_params=_params("arbitrary", "arbitrary", "arbitrary"),
        name="retention",
    )(log_g, qkvg, qkvg, qkvg, qkvg, cos_t, sin_t, gn_g.reshape(1, 2 * D), s0)


def _rotation_tables(pos, dk):
    inv = 1.0 / (10000.0 ** jnp.linspace(0.0, 1.0, dk // 2, dtype=F32))
    ang = pos.astype(F32)[:, None] * inv
    cos = jnp.repeat(jnp.cos(ang), 2, axis=1)
    sin = jnp.stack([-jnp.sin(ang), jnp.sin(ang)], axis=-1).reshape(pos.shape[0], dk)
    return cos, sin


def _shift_mix_kernel(x_ref, xp_ref, mu_ref, *o_refs):
    x = x_ref[...]
    xx = xp_ref[...] - x
    for i, o_ref in enumerate(o_refs):
        o_ref[...] = (x + xx * mu_ref[i:i + 1, :]).astype(o_ref.dtype)


def shift_mix(x, x_prev, mu, *, tm):
    M, D = x.shape
    n = mu.shape[0]
    tile = pl.BlockSpec((tm, D), lambda i: (i, 0))
    return pl.pallas_call(
        _shift_mix_kernel,
        grid=(M // tm,),
        in_specs=[tile, tile, pl.BlockSpec((n, D), lambda i: (0, 0))],
        out_specs=tuple(tile for _ in range(n)),
        out_shape=tuple(jax.ShapeDtypeStruct((M, D), BF16) for _ in range(n)),
        compiler_params=_params("arbitrary"),
        name="rwkv_shift_mix",
    )(x, x_prev, mu)


def _split3(x):
    hi = x.astype(BF16)
    r1 = x - hi.astype(F32)
    mid = r1.astype(BF16)
    lo = (r1 - mid.astype(F32)).astype(BF16)
    return hi, mid, lo


def _dot2(a, b, dn):
    a_hi = a.astype(BF16)
    a_lo = (a - a_hi.astype(F32)).astype(BF16)
    b_hi = b.astype(BF16)
    b_lo = (b - b_hi.astype(F32)).astype(BF16)
    dg = functools.partial(lax.dot_general, dimension_numbers=dn, preferred_element_type=F32)
    return dg(a_hi, b_hi) + (dg(a_hi, b_lo) + dg(a_lo, b_hi))


_DN_NN = (((2,), (1,)), ((0,), (0,)))
_DN_NT = (((2,), (2,)), ((0,), (0,)))
_DN_TN = (((1,), (1,)), ((0,), (0,)))


def _rwkv_kernel(r_ref, k_ref, v_ref, lw_ref, a_ref, g_ref, kk_ref, ka_ref, rk_ref, gng_ref, gnb_ref, s0_ref,
                 y_ref, s_out_ref, s_acc):
    c = pl.program_id(2)
    n_chunks = pl.num_programs(2)
    C = r_ref.shape[0]
    N = W_HEAD
    G = r_ref.shape[1] // N

    @pl.when(c == 0)
    def _():
        s_acc[...] = s0_ref[0]

    ti = lax.broadcasted_iota(jnp.int32, (C, C), 0)
    tj = lax.broadcasted_iota(jnp.int32, (C, C), 1)
    strict = ti > tj
    incl = ti >= tj
    tri16 = jnp.broadcast_to(jnp.where(incl, 1.0, 0.0).astype(BF16), (G, C, C))
    n_double = max(1, (C - 1).bit_length())

    heads = lambda ref: jnp.stack([ref[:, hh * N:(hh + 1) * N] for hh in range(G)])
    r, k, v, lw, a = heads(r_ref), heads(k_ref), heads(v_ref), heads(lw_ref), heads(a_ref)
    kk = k * heads(kk_ref)
    kk = kk / jnp.maximum(jnp.sqrt(jnp.sum(kk * kk, axis=-1, keepdims=True)), 1e-12)
    kh = k * (1.0 + (a - 1.0) * heads(ka_ref))
    av = -kk
    bv = kk * a
    dg = functools.partial(lax.dot_general, dimension_numbers=_DN_NN, preferred_element_type=F32)
    hi, mid, lo = _split3(lw)
    cum = dg(tri16, hi) + dg(tri16, mid) + dg(tri16, lo)
    e_in = jnp.exp(cum)
    e_ex = jnp.exp(cum - lw)
    e_neg = jnp.exp(-cum)
    at = av * e_ex
    rt = r * e_in
    bt = bv * e_neg
    kt = kh * e_neg
    s0 = s_acc[...]
    lh = jnp.concatenate([at, rt], axis=1)
    m_b = _dot2(lh, bt, _DN_NT)
    m_k = _dot2(lh, kt, _DN_NT)
    z = _dot2(lh, s0, _DN_NT)
    a_ab = jnp.where(strict, m_b[:, :C], 0.0)
    a_ak = jnp.where(strict, m_k[:, :C], 0.0)
    a_rb = jnp.where(incl, m_b[:, C:], 0.0)
    a_rk = jnp.where(incl, m_k[:, C:], 0.0)
    u = z[:, :C] + _dot2(a_ak, v, _DN_NN)
    pw = a_ab
    for it in range(n_double):
        u = u + _dot2(pw, u, _DN_NN)
        if it + 1 < n_double:
            pw = _dot2(pw, pw, _DN_NN)
    y = z[:, C:] + _dot2(a_rb, u, _DN_NN) + _dot2(a_rk, v, _DN_NN)
    wc = e_in[:, C - 1:C, :]
    s_acc[...] = s0 * wc + _dot2(u, bt * wc, _DN_TN) + _dot2(v, kt * wc, _DN_TN)

    mu = jnp.mean(y, axis=-1, keepdims=True)
    yc = y - mu
    var = jnp.mean(yc * yc, axis=-1, keepdims=True)
    yn = yc * lax.rsqrt(var + W_GN_EPS)
    bonus = jnp.sum(r * kh * heads(rk_ref), axis=-1, keepdims=True) * v
    out = (yn * heads(gng_ref) + heads(gnb_ref) + bonus) * heads(g_ref)
    for hh in range(G):
        y_ref[:, hh * N:(hh + 1) * N] = out[hh].astype(y_ref.dtype)

    @pl.when(c == n_chunks - 1)
    def _():
        s_out_ref[0] = s_acc[...]


def rwkv_recurrence(r, k, v, lw, a, g, k_k, k_a, r_k, gn_g, gn_b, s0, *, B, L, C):
    M, D = r.shape
    N = W_HEAD
    H = D // N
    G = RWKV_HEADS_PER_STEP
    assert L % C == 0 and M == B * L and H % G == 0
    nC = L // C
    tok = pl.BlockSpec((C, G * N), lambda b, hg, c: (b * nC + c, hg))
    row = pl.BlockSpec((1, G * N), lambda b, hg, c: (0, hg))
    sspec = pl.BlockSpec((1, G, N, N), lambda b, hg, c: (b, hg, 0, 0))
    rows = [p.reshape(1, D) for p in (k_k, k_a, r_k, gn_g, gn_b)]
    return pl.pallas_call(
        _rwkv_kernel,
        grid=(B, H // G, nC),
        in_specs=[tok] * 6 + [row] * 5 + [sspec],
        out_specs=(tok, sspec),
        out_shape=(jax.ShapeDtypeStruct((M, D), BF16), jax.ShapeDtypeStruct(s0.shape, F32)),
        scratch_shapes=[pltpu.VMEM((G, N, N), F32)],
        compiler_params=_params("arbitrary", "arbitrary", "arbitrary"),
        name="rwkv_recurrence",
    )(r, k, v, lw, a, g, *rows, s0)


def _tiles(M):
    if M >= 1024:
        return 1024, 512, 256
    return M, M, M


def _layer_tail(x32, pre1, p16, w, *, M):
    tm, tm_small, tr = _tiles(M)
    D = x32.shape[1]
    F = w["ffn_out"].shape[0]
    h32, h16 = layer_norm(pre1, w["ln1_g"], w["ln1_b"], tm=tr)
    act = matmul(h16, [(w["ffn_in"], 0), (w["ffn_in"], F)], n=F, tm=tm, tn=256,
                 epi=lambda gt, up: gt * jax.nn.sigmoid(gt) * up, out_dtypes=(BF16,), name="ffn_in")
    pp = matmul(p16, [(w["ple_proj"], 0)], n=D, tm=tm, tn=512, name="ple_proj")
    ple = matmul(h16, [(w["ple_gate"], 0)], n=D, tm=tm, tn=512, extras=(pp,),
                 epi=lambda acc, ppt: jax.nn.sigmoid(acc) * ppt, name="ple_gate")
    pre2 = matmul(act, [(w["ffn_out"], 0)], n=D, tm=tm_small, tn=256, extras=(ple, h32),
                  epi=lambda acc, plet, ht: ALPHA * ht + acc + plet, name="ffn_out")
    return layer_norm(pre2, w["ln2_g"], w["ln2_b"], tm=tr)


def _moba_layer(x32, x16, w, rel_bias, *, M, prompt, B, S=None, k_pool=None, v_pool=None, layer=None,
                page_table=None):
    tm, _, _ = _tiles(M)
    D = x32.shape[1]
    q, k, v = [matmul(x16, [(w["qkv"], i * D)], n=D, tm=tm, tn=512, name="moba_qkv") for i in range(3)]
    if prompt:
        att = moba_prompt(q, k, v, rel_bias, B=B, S=S, H=A_HEADS)
    else:
        att = moba_sample(q[:B], k[:B], v[:B], k_pool, v_pool, layer, page_table, rel_bias, H=A_HEADS)
        att = jnp.pad(att, ((0, M - B), (0, 0)))
    pre1 = matmul(att, [(w["o"], 0)], n=D, tm=tm, tn=512, extras=(x32,),
                  epi=lambda acc, xt: ALPHA * xt + acc, name="moba_o")
    return pre1, k, v


def _ret_layer(x32, x16, w, s0, pos0, *, M, B, L):
    tm, tm_small, _ = _tiles(M)
    D = x32.shape[1]
    dk = D // RET_HEADS
    qkvg = matmul(x16, [(w["in"], 0)], n=6 * D, tm=tm, tn=512, name="ret_in")
    if L % RET_CHUNK == 0:
        C, c_real, Lp = RET_CHUNK, RET_CHUNK, L
        cos_t, sin_t = _rotation_tables(pos0 + jnp.arange(L, dtype=jnp.int32), dk)
        y, s_new = retention(qkvg, s0, w["gn_g"], cos_t, sin_t, B=B, L=L, H=RET_HEADS, C=C, c_real=c_real)
    else:
        C, c_real, Lp = SAMPLE_ROWS, L, SAMPLE_ROWS
        assert L == 1
        cos_t, sin_t = _rotation_tables(pos0 + jnp.arange(Lp, dtype=jnp.int32), dk)
        padded = jnp.pad(qkvg[:B].reshape(B, L, 6 * D), ((0, 0), (0, Lp - L), (0, 0))).reshape(B * Lp, 6 * D)
        y, s_new = retention(padded, s0, w["gn_g"], cos_t, sin_t, B=B, L=Lp, H=RET_HEADS, C=C, c_real=c_real)
        y = jnp.pad(y.reshape(B, Lp, 2 * D)[:, 0], ((0, M - B), (0, 0)))
    pre1 = matmul(y, [(w["o"], 0)], n=D, tm=tm_small, tn=512, extras=(x32,),
                  epi=lambda acc, xt: ALPHA * xt + acc, name="ret_o")
    return pre1, s_new


def _rwkv_layer(x32, x_prev, w, s0, *, M, B, L):
    tm, _, tr = _tiles(M)
    D = x32.shape[1]
    xr, xk, xv, xw, xa, xg = shift_mix(x32, x_prev, w["mu"], tm=min(tr, 128))
    r = matmul(xr, [(w["rkv"][0], 0)], n=D, tm=tm, tn=512, name="rwkv_r")
    k = matmul(xk, [(w["rkv"][1], 0)], n=D, tm=tm, tn=512, name="rwkv_k")
    v = matmul(xv, [(w["rkv"][2], 0)], n=D, tm=tm, tn=512, name="rwkv_v")
    lora = w["w1"].shape[1]
    wmid = matmul(xw, [(w["w1"], 0)], n=lora, tm=tm, tn=lora, epi=jnp.tanh, out_dtypes=(BF16,), name="rwkv_w1")

    def log_decay(acc, w0):
        z = -(w0 + acc)
        softplus = jnp.maximum(z, 0.0) + jnp.log1p(jnp.exp(-jnp.abs(z)))
        return -jnp.exp(-softplus - 0.5)

    lw = matmul(wmid, [(w["w2"], 0)], n=D, tm=tm, tn=512, extras=((w["w0"], 0),), epi=log_decay, name="rwkv_w2")
    amid = matmul(xa, [(w["a1"], 0)], n=lora, tm=tm, tn=lora, out_dtypes=(BF16,), name="rwkv_a1")
    a = matmul(amid, [(w["a2"], 0)], n=D, tm=tm, tn=512, extras=((w["a0"], 0),),
               epi=lambda acc, a0: jax.nn.sigmoid(a0 + acc), name="rwkv_a2")
    gl = w["g1"].shape[1]
    gmid = matmul(xg, [(w["g1"], 0)], n=gl, tm=tm, tn=gl, epi=jax.nn.sigmoid, out_dtypes=(BF16,), name="rwkv_g1")
    g = matmul(gmid, [(w["g2"], 0)], n=D, tm=tm, tn=512, name="rwkv_g2")
    params = (w["k_k"], w["k_a"], w["r_k"], w["gn_g"], w["gn_b"])
    if L % RWKV_CHUNK == 0:
        y, s_new = rwkv_recurrence(r, k, v, lw, a, g, *params, s0, B=B, L=L, C=RWKV_CHUNK)
    else:
        assert L == 1
        Lp = SAMPLE_ROWS
        pad = lambda t: jnp.pad(t[:B].reshape(B, L, D), ((0, 0), (0, Lp - L), (0, 0))).reshape(B * Lp, D)
        y, s_new = rwkv_recurrence(*[pad(t) for t in (r, k, v, lw, a, g)], *params, s0, B=B, L=Lp, C=Lp)
        y = jnp.pad(y.reshape(B, Lp, D)[:, 0], ((0, M - B), (0, 0)))
    pre1 = matmul(y, [(w["o"], 0)], n=D, tm=tm, tn=512, extras=(x32,),
                  epi=lambda acc, xt: ALPHA * xt + acc, name="rwkv_o")
    return pre1, s_new


def kernel(x_prompt, x_sample, p_prompt, p_sample, cache_moba_k, cache_moba_v, page_table, state_ret, state_wkv, state_shift, ln1_g, ln1_b, ln2_g, ln2_b, w_ffn_in, w_ffn_out, w_ple_gate, w_ple_proj, rel_bias, a_w_qkv, a_w_o, r_w_in, r_gn_g, r_w_o, c_mu, c_w_rkv, c_w0, c_w1, c_w2, c_a0, c_a1, c_a2, c_g1, c_g2, c_k_k, c_k_a, c_r_k, c_gn_g, c_gn_b, c_w_o):
    Bp, Sp, D = x_prompt.shape
    Bs, Ts, _ = x_sample.shape
    assert Ts == 1 and Bs <= SAMPLE_ROWS
    Mp, Ms = Bp * Sp, SAMPLE_ROWS
    n_pool = cache_moba_k.shape[1]
    past = page_table.shape[1] * PAGE_SIZE
    bf = lambda t: t.astype(BF16)
    pad_rows = lambda t: jnp.pad(t, ((0, Ms - Bs), (0, 0)))

    xp32 = x_prompt.reshape(Mp, D)
    xs32 = pad_rows(x_sample.reshape(Bs, D))
    xp16, xs16 = bf(xp32), bf(xs32)
    k_pool = cache_moba_k.reshape(cache_moba_k.shape[0], n_pool, PAGE_SIZE, D)
    v_pool = cache_moba_v.reshape(cache_moba_v.shape[0], n_pool, PAGE_SIZE, D)
    gpad = W_GATE_LORA_PAD - c_g1.shape[-1]

    kp_l, vp_l, ks_l, vs_l, rp_l, rs_l, wp_l, ws_l, hp_l, hs_l = ([] for _ in range(10))
    for i in range(DEPTH):
        kind, j = i % 3, i // 3
        tail = dict(ln1_g=ln1_g[i], ln1_b=ln1_b[i], ln2_g=ln2_g[i], ln2_b=ln2_b[i], ffn_in=bf(w_ffn_in[i]),
                    ffn_out=bf(w_ffn_out[i]), ple_gate=bf(w_ple_gate[i]), ple_proj=bf(w_ple_proj[i]))
        if kind == 0:
            w = dict(qkv=bf(a_w_qkv[j]), o=bf(a_w_o[j]))
            pre_p, kp, vp = _moba_layer(xp32, xp16, w, rel_bias, M=Mp, prompt=True, B=Bp, S=Sp)
            pre_s, ks, vs = _moba_layer(xs32, xs16, w, rel_bias, M=Ms, prompt=False, B=Bs, k_pool=k_pool,
                                        v_pool=v_pool, layer=j, page_table=page_table)
            kp_l.append(kp.reshape(Bp, Sp, A_HEADS, D // A_HEADS))
            vp_l.append(vp.reshape(Bp, Sp, A_HEADS, D // A_HEADS))
            ks_l.append(ks[:Bs].reshape(Bs, Ts, A_HEADS, D // A_HEADS))
            vs_l.append(vs[:Bs].reshape(Bs, Ts, A_HEADS, D // A_HEADS))
        elif kind == 1:
            w = dict(o=bf(r_w_o[j]), gn_g=r_gn_g[j])
            w["in"] = bf(r_w_in[j])
            s0 = jnp.zeros((Bp,) + state_ret.shape[2:], state_ret.dtype)
            pre_p, sp = _ret_layer(xp32, xp16, w, s0, 0, M=Mp, B=Bp, L=Sp)
            pre_s, ss = _ret_layer(xs32, xs16, w, state_ret[j], past, M=Ms, B=Bs, L=Ts)
            rp_l.append(sp)
            rs_l.append(ss)
        else:
            w = dict(mu=c_mu[j], rkv=bf(c_w_rkv[j]), w0=c_w0[j].reshape(1, D), w1=bf(c_w1[j]), w2=bf(c_w2[j]),
                     a0=c_a0[j].reshape(1, D), a1=bf(c_a1[j]), a2=bf(c_a2[j]),
                     g1=bf(jnp.pad(c_g1[j], ((0, 0), (0, gpad)))), g2=bf(jnp.pad(c_g2[j], ((0, gpad), (0, 0)))),
                     k_k=c_k_k[j], k_a=c_k_a[j], r_k=c_r_k[j], gn_g=c_gn_g[j], gn_b=c_gn_b[j], o=bf(c_w_o[j]))
            xp3 = xp32.reshape(Bp, Sp, D)
            xprev_p = jnp.concatenate([jnp.zeros((Bp, 1, D), F32), xp3[:, :-1]], axis=1).reshape(Mp, D)
            xprev_s = pad_rows(state_shift[j])
            wk0 = jnp.zeros((Bp,) + state_wkv.shape[2:], state_wkv.dtype)
            pre_p, wkp = _rwkv_layer(xp32, xprev_p, w, wk0, M=Mp, B=Bp, L=Sp)
            pre_s, wks = _rwkv_layer(xs32, xprev_s, w, state_wkv[j], M=Ms, B=Bs, L=Ts)
            wp_l.append(wkp)
            ws_l.append(wks)
            hp_l.append(xp3[:, -1])
            hs_l.append(xs32[:Bs])
        xp32, xp16 = _layer_tail(xp32, pre_p, bf(p_prompt[i].reshape(Mp, -1)), tail, M=Mp)
        xs32, xs16 = _layer_tail(xs32, pre_s, bf(pad_rows(p_sample[i].reshape(Bs, -1))), tail, M=Ms)
    return (xp32.reshape(Bp, Sp, D), xs32[:Bs].reshape(Bs, Ts, D),
            jnp.stack(kp_l), jnp.stack(vp_l), jnp.stack(ks_l), jnp.stack(vs_l),
            jnp.stack(rp_l), jnp.stack(rs_l), jnp.stack(wp_l), jnp.stack(ws_l),
            jnp.stack(hp_l), jnp.stack(hs_l))
```

```python
import functools
import math

import numpy as np
import jax
import jax.numpy as jnp
from jax import lax
from jax.experimental import pallas as pl
from jax.experimental.pallas import tpu as pltpu

F32 = jnp.float32
BF16 = jnp.bfloat16

DEPTH = 4
LN_EPS = 1e-5
ALPHA = (2 * DEPTH) ** 0.25
A_HEADS = 32
MOBA_BLOCK = 256
MOBA_TOPK = 3
PAGE_SIZE = 128
REL_BUCKETS = 32
REL_MAX_DIST = 128
RET_HEADS = 16
RET_CHUNK = 128
RET_GN_EPS = 1e-5
W_HEAD = 64
W_GN_EPS = 64e-5
W_GATE_LORA_PAD = 512
NEG_INF = -1e30

V7X_VMEM_LIMIT_BYTES = 56 * 1024 * 1024
LANES = 128
F32_SUBLANES = 8
CACHE_HEAD_GROUP = F32_SUBLANES

MOBA_HEADS_PER_STEP = 4
RET_HEADS_PER_STEP = 4
RWKV_CHUNK = 64
RWKV_HEADS_PER_STEP = 16
SAMPLE_ROWS = 16


def _t5_bucket_thresholds():
    n = np.arange(0, 4 * REL_MAX_DIST)
    max_exact = REL_BUCKETS // 2
    nf = np.maximum(n, 1).astype(np.float32)
    large = max_exact + (np.log(nf / np.float32(max_exact)) / np.float32(math.log(REL_MAX_DIST / max_exact))
                         * np.float32(REL_BUCKETS - max_exact)).astype(np.int32)
    bucket = np.where(n < max_exact, n, np.minimum(large, REL_BUCKETS - 1))
    assert sorted(set(bucket.tolist())) == list(range(REL_BUCKETS))
    return tuple(int(np.argmax(bucket >= b)) for b in range(REL_BUCKETS))


T5_THRESHOLDS = _t5_bucket_thresholds()


def _params(*sem):
    return pltpu.CompilerParams(dimension_semantics=sem, vmem_limit_bytes=V7X_VMEM_LIMIT_BYTES)


def _bias_from_distance(dist, table_ref, h):
    t = jnp.full(dist.shape, table_ref[0, h], F32)
    for b in range(1, REL_BUCKETS):
        t = jnp.where(dist >= T5_THRESHOLDS[b], table_ref[b, h], t)
    return t


def _mm_kernel(epi, n_w, n_ex, x_ref, *refs):
    w_refs = refs[:n_w]
    ex_refs = refs[n_w:n_w + n_ex]
    o_refs = refs[n_w + n_ex:]
    x = x_ref[...]
    accs = [jnp.dot(x, w[...], preferred_element_type=F32) for w in w_refs]
    outs = epi(*accs, *[e[...] for e in ex_refs])
    if not isinstance(outs, tuple):
        outs = (outs,)
    for o_ref, o in zip(o_refs, outs):
        o_ref[...] = o.astype(o_ref.dtype)


def matmul(x, ws, *, n, tm, tn, epi=None, extras=(), out_dtypes=(F32,), name="mm"):
    M, K = x.shape
    assert M % tm == 0 and n % tn == 0, (M, tm, n, tn)
    if epi is None:
        epi = lambda a: a
    in_specs = [pl.BlockSpec((tm, K), lambda i, j: (i, 0))]
    args = [x]
    for w, col0 in ws:
        w, lead = w if isinstance(w, tuple) else (w, ())
        assert w.shape[-2] == K and col0 % tn == 0 and w.ndim == len(lead) + 2
        in_specs.append(pl.BlockSpec((None,) * len(lead) + (K, tn),
                                     lambda i, j, cb=col0 // tn, lead=lead: lead + (0, j + cb)))
        args.append(w)
    for e in extras:
        if isinstance(e, tuple):
            row, col0 = e
            assert row.shape[0] == 1 and col0 % tn == 0
            in_specs.append(pl.BlockSpec((1, tn), lambda i, j, cb=col0 // tn: (0, j + cb)))
            args.append(row)
        else:
            assert e.shape == (M, n)
            in_specs.append(pl.BlockSpec((tm, tn), lambda i, j: (i, j)))
            args.append(e)
    out_shape = tuple(jax.ShapeDtypeStruct((M, n), dt) for dt in out_dtypes)
    out_specs = tuple(pl.BlockSpec((tm, tn), lambda i, j: (i, j)) for _ in out_dtypes)
    res = pl.pallas_call(
        functools.partial(_mm_kernel, epi, len(ws), len(extras)),
        grid=(M // tm, n // tn),
        in_specs=in_specs,
        out_specs=out_specs,
        out_shape=out_shape,
        compiler_params=_params("arbitrary", "arbitrary"),
        name=name,
    )(*args)
    return res if len(out_dtypes) > 1 else res[0]


def _ln_kernel(x_ref, g_ref, b_ref, o32_ref, o16_ref):
    x = x_ref[...]
    mu = jnp.mean(x, axis=-1, keepdims=True)
    xc = x - mu
    var = jnp.mean(xc * xc, axis=-1, keepdims=True)
    y = xc * lax.rsqrt(var + LN_EPS) * g_ref[...] + b_ref[...]
    o32_ref[...] = y
    o16_ref[...] = y.astype(BF16)


def layer_norm(x, g, b, *, tm):
    M, D = x.shape
    row = pl.BlockSpec((1, D), lambda i: (0, 0))
    tile = pl.BlockSpec((tm, D), lambda i: (i, 0))
    return pl.pallas_call(
        _ln_kernel,
        grid=(M // tm,),
        in_specs=[tile, row, row],
        out_specs=(tile, tile),
        out_shape=(jax.ShapeDtypeStruct((M, D), F32), jax.ShapeDtypeStruct((M, D), BF16)),
        compiler_params=_params("arbitrary"),
        name="layer_norm",
    )(x, g.reshape(1, D), b.reshape(1, D))


def _moba_prompt_kernel(nb, G, tab_ref, q_ref, k_ref, v_ref, o_ref, bown, bprev, kmean, m_s, l_s, acc_s):
    hg = pl.program_id(0)
    b = pl.program_id(1)
    qb = pl.program_id(2)
    blk = MOBA_BLOCK
    n_cand = nb - 1
    k_sel_n = min(MOBA_TOPK, n_cand)
    hd = q_ref.shape[1] // G
    a_scale = hd ** -0.5
    h0 = hg * G
    row = lax.broadcasted_iota(jnp.int32, (blk, blk), 0)
    col = lax.broadcasted_iota(jnp.int32, (blk, blk), 1)
    heads = lambda x: jnp.stack([x[:, g * hd:(g + 1) * hd] for g in range(G)])

    @pl.when((b == 0) & (qb == 0))
    def _():
        for g in range(G):
            bown[g] = _bias_from_distance(jnp.maximum(row - col, 0), tab_ref, h0 + g)
            bprev[g] = _bias_from_distance(row - col + blk, tab_ref, h0 + g)

    @pl.when(qb == 0)
    def _():
        kmean[...] = jnp.zeros(kmean.shape, F32)
        for n in range(n_cand):
            mean_n = jnp.mean(k_ref[n * blk:(n + 1) * blk, :], axis=0, keepdims=True)
            for g in range(G):
                kmean[g, n:n + 1, :] = mean_n[:, g * hd:(g + 1) * hd]

    q = heads(q_ref[...])
    q16 = q.astype(BF16)
    dn_qk = (((2,), (2,)), ((0,), (0,)))
    dn_pv = (((2,), (1,)), ((0,), (0,)))

    start = pl.multiple_of(qb * blk, blk)
    k_own = heads(k_ref[pl.ds(start, blk), :]).astype(BF16)
    v_own = heads(v_ref[pl.ds(start, blk), :]).astype(BF16)
    s = lax.dot_general(q16, k_own, dn_qk, preferred_element_type=F32) * a_scale + bown[...]
    s = jnp.where(row >= col, s, NEG_INF)
    m0 = jnp.max(s, axis=-1, keepdims=True)
    p = jnp.exp(s - m0)
    m_s[...] = m0
    l_s[...] = jnp.sum(p, axis=-1, keepdims=True)
    acc_s[...] = lax.dot_general(p.astype(BF16), v_own, dn_pv, preferred_element_type=F32)

    if n_cand > 0:
        ncols = kmean.shape[1]
        gate = lax.dot_general(q, kmean[...], dn_qk, precision=lax.Precision.HIGHEST,
                               preferred_element_type=F32)
        ncol = lax.broadcasted_iota(jnp.int32, (G, blk, ncols), 2)
        gate = jnp.where(ncol < qb, gate, NEG_INF)
        rank = jnp.zeros((G, blk, ncols), F32)
        for m in range(n_cand):
            gm = gate[:, :, m:m + 1]
            ahead = (gm > gate) | ((gm == gate) & (ncol > m))
            rank = rank + jnp.where(ahead, 1.0, 0.0)
        sel = jnp.where((rank < k_sel_n) & (ncol < qb), 1.0, 0.0)
        gid = lax.broadcasted_iota(jnp.int32, (G, 1, 1), 0)
        far_bias = jnp.full((G, 1, 1), tab_ref[REL_BUCKETS - 1, h0], F32)
        for g in range(1, G):
            far_bias = jnp.where(gid == g, tab_ref[REL_BUCKETS - 1, h0 + g], far_bias)

        for n in range(n_cand):
            @pl.when(n < qb)
            def _(n=n):
                kn = heads(k_ref[n * blk:(n + 1) * blk, :]).astype(BF16)
                vn = heads(v_ref[n * blk:(n + 1) * blk, :]).astype(BF16)
                bias = jnp.where(n == qb - 1, bprev[...], far_bias)
                sn = lax.dot_general(q16, kn, dn_qk, preferred_element_type=F32) * a_scale + bias
                sn = jnp.where(sel[:, :, n:n + 1] > 0.5, sn, NEG_INF)
                m_old = m_s[...]
                m_new = jnp.maximum(m_old, jnp.max(sn, axis=-1, keepdims=True))
                scale_old = jnp.exp(m_old - m_new)
                pn = jnp.exp(sn - m_new)
                l_s[...] = scale_old * l_s[...] + jnp.sum(pn, axis=-1, keepdims=True)
                acc_s[...] = scale_old * acc_s[...] + lax.dot_general(pn.astype(BF16), vn, dn_pv,
                                                                      preferred_element_type=F32)
                m_s[...] = m_new

    out = acc_s[...] / l_s[...]
    for g in range(G):
        o_ref[:, g * hd:(g + 1) * hd] = out[g].astype(o_ref.dtype)


def moba_prompt(q, k, v, rel_bias, *, B, S, H):
    M, D = q.shape
    hd = D // H
    blk = MOBA_BLOCK
    G = math.gcd(H, MOBA_HEADS_PER_STEP)
    assert S % blk == 0 and M == B * S and T5_THRESHOLDS[-1] <= blk
    nb = S // blk
    ncols = max(8, -(-(nb - 1) // 8) * 8)
    qspec = pl.BlockSpec((blk, G * hd), lambda hg, b, qb: (b * nb + qb, hg))
    kvspec = pl.BlockSpec((S, G * hd), lambda hg, b, qb: (b, hg))
    return pl.pallas_call(
        functools.partial(_moba_prompt_kernel, nb, G),
        grid=(H // G, B, nb),
        in_specs=[pl.BlockSpec(memory_space=pltpu.SMEM), qspec, kvspec, kvspec],
        out_specs=qspec,
        out_shape=jax.ShapeDtypeStruct((M, D), BF16),
        scratch_shapes=[pltpu.VMEM((G, blk, blk), F32), pltpu.VMEM((G, blk, blk), F32),
                        pltpu.VMEM((G, ncols, hd), F32), pltpu.VMEM((G, blk, 1), F32),
                        pltpu.VMEM((G, blk, 1), F32), pltpu.VMEM((G, blk, hd), F32)],
        compiler_params=_params("arbitrary", "arbitrary", "arbitrary"),
        name="moba_prompt",
    )(rel_bias, q, k, v)


def _moba_select_kernel(n_blocks, pt_ref, pa_ref, pb_ref, q_ref, sel_ref, sums):
    j = pl.program_id(1)
    sums[j] = jnp.sum(pa_ref[0, 0], axis=0) + jnp.sum(pb_ref[0, 0], axis=0)

    @pl.when(j == n_blocks - 1)
    def _():
        prod = (sums[...] * (1.0 / MOBA_BLOCK)) * q_ref[...]
        gate = jnp.sum(prod, axis=-1, keepdims=True)
        blk_id = lax.broadcasted_iota(jnp.int32, gate.shape, 0)
        rank = jnp.zeros(gate.shape, F32)
        for m in range(n_blocks):
            gm = gate[m:m + 1]
            ahead = (gm > gate) | ((gm == gate) & (blk_id > m))
            rank = rank + jnp.where(ahead, 1.0, 0.0)
        for r in range(sel_ref.shape[1]):
            sel_ref[0, r] = jnp.sum(jnp.where(rank == float(r), blk_id, 0), axis=0)


def _moba_sample_attn_kernel(n_sel_pages, past, pt_ref, sel_ref, tab_ref, q_ref, kn_ref, vn_ref, *refs):
    k_refs = refs[:n_sel_pages]
    v_refs = refs[n_sel_pages:2 * n_sel_pages]
    o_ref = refs[2 * n_sel_pages]
    b = pl.program_id(0)
    h = pl.program_id(1)
    n_heads = pl.num_programs(1)
    hd = q_ref.shape[-1]
    a_scale = hd ** -0.5
    pages_per_block = MOBA_BLOCK // PAGE_SIZE
    hsel = lax.broadcasted_iota(jnp.int32, (1, CACHE_HEAD_GROUP, 1), 1) == h % CACHE_HEAD_GROUP
    pick = lambda ref: jnp.sum(jnp.where(hsel, ref[0, 0], 0.0), axis=1)
    q = q_ref[0]
    s_own = jnp.sum(q * kn_ref[0], axis=-1, keepdims=True) * a_scale + tab_ref[0, h]
    r = lax.broadcasted_iota(jnp.int32, (PAGE_SIZE, 1), 0)
    scores = []
    for t in range(n_sel_pages):
        blk = sel_ref[(b * n_heads + h) * MOBA_TOPK + t // pages_per_block]
        pos0 = blk * MOBA_BLOCK + (t % pages_per_block) * PAGE_SIZE
        dist = past - (pos0 + r)
        bias = _bias_from_distance(dist, tab_ref, h)
        scores.append(jnp.sum(pick(k_refs[t]) * q, axis=-1, keepdims=True) * a_scale + bias)
    m = s_own
    for s in scores:
        m = jnp.maximum(m, jnp.max(s, axis=0, keepdims=True))
    p_own = jnp.exp(s_own - m)
    l = p_own
    acc = p_own * vn_ref[0]
    for t, s in enumerate(scores):
        p = jnp.exp(s - m)
        l = l + jnp.sum(p, axis=0, keepdims=True)
        acc = acc + jnp.sum(p * pick(v_refs[t]), axis=0, keepdims=True)
    o_ref[0] = (acc / l).astype(o_ref.dtype)


def moba_sample(q, k_new, v_new, k_pool, v_pool, layer, page_table, rel_bias, *, H):
    rows, D = q.shape
    B, n_pages = page_table.shape
    assert rows == B
    hd = D // H
    past = n_pages * PAGE_SIZE
    ppb = MOBA_BLOCK // PAGE_SIZE
    assert past % MOBA_BLOCK == 0 and H % CACHE_HEAD_GROUP == 0
    n_blocks = past // MOBA_BLOCK
    assert n_blocks >= MOBA_TOPK
    pt = page_table.reshape(-1)
    q3 = q.reshape(rows, 1, D)

    page = lambda half: pl.BlockSpec(
        (1, 1, PAGE_SIZE, H, hd), lambda b, j, pt_ref: (layer, pt_ref[b * n_pages + ppb * j + half], 0, 0, 0))
    sel = pl.pallas_call(
        functools.partial(_moba_select_kernel, n_blocks),
        grid_spec=pltpu.PrefetchScalarGridSpec(
            num_scalar_prefetch=1,
            grid=(B, n_blocks),
            in_specs=[page(0), page(1), pl.BlockSpec((1, H, hd), lambda b, j, pt_ref: (b, 0, 0))],
            out_specs=pl.BlockSpec((1, MOBA_TOPK, H, 1), lambda b, j, pt_ref: (b, 0, 0, 0)),
            scratch_shapes=[pltpu.VMEM((n_blocks, H, hd), F32)]),
        out_shape=jax.ShapeDtypeStruct((B, MOBA_TOPK, H, 1), jnp.int32),
        compiler_params=_params("arbitrary", "arbitrary"),
        name="moba_sample_select",
    )(pt, k_pool, k_pool, q.reshape(B, H, hd))
    sel_flat = jnp.transpose(sel[..., 0], (0, 2, 1)).reshape(-1)

    n_sel_pages = MOBA_TOPK * ppb

    def sel_page(t):
        return pl.BlockSpec(
            (1, 1, PAGE_SIZE, CACHE_HEAD_GROUP, hd),
            lambda b, h, pt_ref, sel_ref: (
                layer, pt_ref[b * n_pages + ppb * sel_ref[(b * H + h) * MOBA_TOPK + t // ppb] + t % ppb], 0,
                h // CACHE_HEAD_GROUP, 0))

    tok = pl.BlockSpec((1, 1, hd), lambda b, h, pt_ref, sel_ref: (b, 0, h))
    out = pl.pallas_call(
        functools.partial(_moba_sample_attn_kernel, n_sel_pages, past),
        grid_spec=pltpu.PrefetchScalarGridSpec(
            num_scalar_prefetch=2,
            grid=(B, H),
            in_specs=[pl.BlockSpec(memory_space=pltpu.SMEM), tok, tok, tok]
                     + [sel_page(t) for t in range(n_sel_pages)] * 2,
            out_specs=tok),
        out_shape=jax.ShapeDtypeStruct((rows, 1, D), BF16),
        compiler_params=_params("arbitrary", "arbitrary"),
        name="moba_sample_attn",
    )(pt, sel_flat, rel_bias, q3, k_new.reshape(rows, 1, D), v_new.reshape(rows, 1, D),
      *([k_pool] * n_sel_pages), *([v_pool] * n_sel_pages))
    return out.reshape(rows, D)


def _retention_kernel(c_real, lg_ref, q_ref, k_ref, v_ref, g_ref, cos_ref, sin_ref, gn_ref, s0_ref,
                      y_ref, s_out_ref, s_acc):
    hg = pl.program_id(1)
    c = pl.program_id(2)
    n_chunks = pl.num_programs(2)
    C, dk = cos_ref.shape
    G = q_ref.shape[1] // dk
    dv = v_ref.shape[1] // G

    @pl.when(c == 0)
    def _():
        s_acc[...] = s0_ref[0]

    gid = lax.broadcasted_iota(jnp.int32, (G, 1, 1), 0)
    lg = jnp.full((G, 1, 1), lg_ref[hg * G], F32)
    for g in range(1, G):
        lg = jnp.where(gid == g, lg_ref[hg * G + g], lg)
    cos = cos_ref[...]
    sin = sin_ref[...]
    even = (lax.broadcasted_iota(jnp.int32, (C, LANES), 1) % 2) == 0
    heads = lambda x, w: jnp.stack([x[:, g * w:(g + 1) * w] for g in range(G)])

    def rotate(x):
        tiles = []
        for t in range(dk // LANES):
            xt = x[:, t * LANES:(t + 1) * LANES]
            tiles.append(jnp.where(even, pltpu.roll(xt, LANES - 1, 1), pltpu.roll(xt, 1, 1)))
        return x * cos + jnp.concatenate(tiles, axis=1) * sin

    q = jnp.stack([rotate(q_ref[:, g * dk:(g + 1) * dk]) for g in range(G)])
    k = jnp.stack([rotate(k_ref[:, g * dk:(g + 1) * dk]) for g in range(G)]) * (dk ** -0.5)
    v16 = heads(v_ref[...], dv).astype(BF16)
    idx = lax.broadcasted_iota(jnp.int32, (1, C, 1), 1).astype(F32)
    q_dec = jnp.exp((idx + 1.0) * lg)
    k_dec = jnp.exp((c_real - 1.0 - idx) * lg)
    ii = lax.broadcasted_iota(jnp.int32, (1, C, C), 1)
    jj = lax.broadcasted_iota(jnp.int32, (1, C, C), 2)
    diff = (ii - jj).astype(F32)
    dmask = jnp.where(diff >= 0, jnp.exp(jnp.maximum(diff, 0.0) * lg), 0.0)
    dg = lambda a, b, dn: lax.dot_general(a.astype(BF16), b.astype(BF16), dn, preferred_element_type=F32)
    qk = dg(q, k, _DN_NT) * dmask
    s_prev = s_acc[...]
    o = dg(qk, v16, _DN_NN) + dg(q * q_dec, s_prev, _DN_NN)
    kv = dg(k * k_dec, v16, _DN_TN)
    c_dec = jnp.exp(c_real * lg)
    s_new = s_prev * c_dec + kv
    s_acc[...] = s_new

    mu = jnp.mean(o, axis=-1, keepdims=True)
    oc = o - mu
    var = jnp.mean(oc * oc, axis=-1, keepdims=True)
    gate = heads(g_ref[...], dv)
    y = oc * lax.rsqrt(var + RET_GN_EPS) * heads(gn_ref[...], dv) * (gate * jax.nn.sigmoid(gate))
    for g in range(G):
        y_ref[:, g * dv:(g + 1) * dv] = y[g].astype(y_ref.dtype)

    @pl.when(c == n_chunks - 1)
    def _():
        s_out_ref[0] = s_new


def retention(qkvg, s0, gn_g, cos_t, sin_t, *, B, L, H, C, c_real):
    M, N6 = qkvg.shape
    D = N6 // 6
    dk, dv = D // H, 2 * D // H
    assert L % C == 0 and M == B * L
    nC = L // C
    G = math.gcd(H, RET_HEADS_PER_STEP)
    nG = H // G
    log_g = jnp.log1p(-jnp.exp2(-5.0 - jnp.arange(H, dtype=F32)))
    tokspec = lambda width, sec: pl.BlockSpec((C, G * width), lambda b, hg, c: (b * nC + c, sec * nG + hg))
    tabspec = pl.BlockSpec((C, dk), lambda b, hg, c: (c, 0))
    sspec = pl.BlockSpec((1, G, dk, dv), lambda b, hg, c: (b, hg, 0, 0))
    return pl.pallas_call(
        functools.partial(_retention_kernel, float(c_real)),
        grid=(B, nG, nC),
        in_specs=[pl.BlockSpec(memory_space=pltpu.SMEM),
                  tokspec(dk, 0), tokspec(dk, 1), tokspec(dv, 1), tokspec(dv, 2),
                  tabspec, tabspec, pl.BlockSpec((1, G * dv), lambda b, hg, c: (0, hg)), sspec],
        out_specs=(pl.BlockSpec((C, G * dv), lambda b, hg, c: (b * nC + c, hg)), sspec),
        out_shape=(jax.ShapeDtypeStruct((M, 2 * D), BF16), jax.ShapeDtypeStruct(s0.shape, F32)),
        scratch_shapes=[pltpu.VMEM((G, dk, dv), F32)],
        compiler_params=_params("arbitrary", "arbitrary", "arbitrary"),
        name="retention",
    )(log_g, qkvg, qkvg, qkvg, qkvg, cos_t, sin_t, gn_g.reshape(1, 2 * D), s0)


def _rotation_tables(pos, dk):
    inv = 1.0 / (10000.0 ** jnp.linspace(0.0, 1.0, dk // 2, dtype=F32))
    ang = pos.astype(F32)[:, None] * inv
    cos = jnp.repeat(jnp.cos(ang), 2, axis=1)
    sin = jnp.stack([-jnp.sin(ang), jnp.sin(ang)], axis=-1).reshape(pos.shape[0], dk)
    return cos, sin


def _shift_mix_kernel(x_ref, xp_ref, mu_ref, *o_refs):
    x = x_ref[...]
    xx = xp_ref[...] - x
    for i, o_ref in enumerate(o_refs):
        o_ref[...] = (x + xx * mu_ref[i:i + 1, :]).astype(o_ref.dtype)


def shift_mix(x, x_prev, mu, *, tm):
    M, D = x.shape
    n = mu.shape[0]
    tile = pl.BlockSpec((tm, D), lambda i: (i, 0))
    return pl.pallas_call(
        _shift_mix_kernel,
        grid=(M // tm,),
        in_specs=[tile, tile, pl.BlockSpec((n, D), lambda i: (0, 0))],
        out_specs=tuple(tile for _ in range(n)),
        out_shape=tuple(jax.ShapeDtypeStruct((M, D), BF16) for _ in range(n)),
        compiler_params=_params("arbitrary"),
        name="rwkv_shift_mix",
    )(x, x_prev, mu)


def _split3(x):
    hi = x.astype(BF16)
    r1 = x - hi.astype(F32)
    mid = r1.astype(BF16)
    lo = (r1 - mid.astype(F32)).astype(BF16)
    return hi, mid, lo


def _dot16(a, b, dn):
    return lax.dot_general(a.astype(BF16), b.astype(BF16), dimension_numbers=dn, preferred_element_type=F32)


_DN_NN = (((2,), (1,)), ((0,), (0,)))
_DN_NT = (((2,), (2,)), ((0,), (0,)))
_DN_TN = (((1,), (1,)), ((0,), (0,)))


def _rwkv_kernel(r_ref, k_ref, v_ref, lw_ref, a_ref, g_ref, kk_ref, ka_ref, rk_ref, gng_ref, gnb_ref, s0_ref,
                 y_ref, s_out_ref, s_acc):
    c = pl.program_id(2)
    n_chunks = pl.num_programs(2)
    C = r_ref.shape[0]
    N = W_HEAD
    G = r_ref.shape[1] // N

    @pl.when(c == 0)
    def _():
        s_acc[...] = s0_ref[0]

    ti = lax.broadcasted_iota(jnp.int32, (C, C), 0)
    tj = lax.broadcasted_iota(jnp.int32, (C, C), 1)
    strict = ti > tj
    incl = ti >= tj
    tri16 = jnp.broadcast_to(jnp.where(incl, 1.0, 0.0).astype(BF16), (G, C, C))
    n_double = max(1, (C - 1).bit_length())

    heads = lambda ref: jnp.stack([ref[:, hh * N:(hh + 1) * N] for hh in range(G)])
    r, k, v, lw, a = heads(r_ref), heads(k_ref), heads(v_ref), heads(lw_ref), heads(a_ref)
    kk = k * heads(kk_ref)
    kk = kk / jnp.maximum(jnp.sqrt(jnp.sum(kk * kk, axis=-1, keepdims=True)), 1e-12)
    kh = k * (1.0 + (a - 1.0) * heads(ka_ref))
    av = -kk
    bv = kk * a
    dg = functools.partial(lax.dot_general, dimension_numbers=_DN_NN, preferred_element_type=F32)
    hi, mid, lo = _split3(lw)
    cum = dg(tri16, hi) + dg(tri16, mid) + dg(tri16, lo)
    e_in = jnp.exp(cum)
    e_ex = jnp.exp(cum - lw)
    e_neg = jnp.exp(-cum)
    at = av * e_ex
    rt = r * e_in
    bt = bv * e_neg
    kt = kh * e_neg
    s0 = s_acc[...]
    lh = jnp.concatenate([at, rt], axis=1)
    m_b = _dot16(lh, bt, _DN_NT)
    m_k = _dot16(lh, kt, _DN_NT)
    z = _dot16(lh, s0, _DN_NT)
    a_ab = jnp.where(strict, m_b[:, :C], 0.0)
    a_ak = jnp.where(strict, m_k[:, :C], 0.0)
    a_rb = jnp.where(incl, m_b[:, C:], 0.0)
    a_rk = jnp.where(incl, m_k[:, C:], 0.0)
    u = z[:, :C] + _dot16(a_ak, v, _DN_NN)
    pw = a_ab
    for it in range(n_double):
        u = u + _dot16(pw, u, _DN_NN)
        if it + 1 < n_double:
            pw = _dot16(pw, pw, _DN_NN)
    y = z[:, C:] + _dot16(a_rb, u, _DN_NN) + _dot16(a_rk, v, _DN_NN)
    wc = e_in[:, C - 1:C, :]
    s_acc[...] = s0 * wc + _dot16(u, bt * wc, _DN_TN) + _dot16(v, kt * wc, _DN_TN)

    mu = jnp.mean(y, axis=-1, keepdims=True)
    yc = y - mu
    var = jnp.mean(yc * yc, axis=-1, keepdims=True)
    yn = yc * lax.rsqrt(var + W_GN_EPS)
    bonus = jnp.sum(r * kh * heads(rk_ref), axis=-1, keepdims=True) * v
    out = (yn * heads(gng_ref) + heads(gnb_ref) + bonus) * heads(g_ref)
    for hh in range(G):
        y_ref[:, hh * N:(hh + 1) * N] = out[hh].astype(y_ref.dtype)

    @pl.when(c == n_chunks - 1)
    def _():
        s_out_ref[0] = s_acc[...]


def rwkv_recurrence(r, k, v, lw, a, g, k_k, k_a, r_k, gn_g, gn_b, s0, *, B, L, C):
    M, D = r.shape
    N = W_HEAD
    H = D // N
    G = math.gcd(H, RWKV_HEADS_PER_STEP)
    assert L % C == 0 and M == B * L
    nC = L // C
    tok = pl.BlockSpec((C, G * N), lambda b, hg, c: (b * nC + c, hg))
    row = pl.BlockSpec((1, G * N), lambda b, hg, c: (0, hg))
    sspec = pl.BlockSpec((1, G, N, N), lambda b, hg, c: (b, hg, 0, 0))
    rows = [p.reshape(1, D) for p in (k_k, k_a, r_k, gn_g, gn_b)]
    return pl.pallas_call(
        _rwkv_kernel,
        grid=(B, H // G, nC),
        in_specs=[tok] * 6 + [row] * 5 + [sspec],
        out_specs=(tok, sspec),
        out_shape=(jax.ShapeDtypeStruct((M, D), BF16), jax.ShapeDtypeStruct(s0.shape, F32)),
        scratch_shapes=[pltpu.VMEM((G, N, N), F32)],
        compiler_params=_params("arbitrary", "arbitrary", "arbitrary"),
        name="rwkv_recurrence",
    )(r, k, v, lw, a, g, *rows, s0)


def _tiles(M):
    if M >= 1024:
        return 1024, 512, 256
    return M, M, M


TN_WIDE = 1024
TN_FFN = 256


def _mat_shape(w):
    return (w[0] if isinstance(w, tuple) else w).shape[-2:]


def _layer_tail(pre1, p16, w, *, M):
    tm, tm_small, tr = _tiles(M)
    F, D = _mat_shape(w["ffn_out"])
    h32, h16 = layer_norm(pre1, w["ln1_g"], w["ln1_b"], tm=tr)
    act = matmul(h16, [(w["ffn_in"], 0), (w["ffn_in"], F)], n=F, tm=min(M, 2 * tm), tn=TN_FFN,
                 epi=lambda gt, up: gt * jax.nn.sigmoid(gt) * up, out_dtypes=(BF16,), name="ffn_in")
    pp = matmul(p16, [(w["ple_proj"], 0)], n=D, tm=tm, tn=TN_WIDE, name="ple_proj")
    ple = matmul(h16, [(w["ple_gate"], 0)], n=D, tm=tm, tn=512, extras=(pp,),
                 epi=lambda acc, ppt: jax.nn.sigmoid(acc) * ppt, name="ple_gate")
    pre2 = matmul(act, [(w["ffn_out"], 0)], n=D, tm=tm_small, tn=TN_FFN, extras=(ple, h32),
                  epi=lambda acc, plet, ht: ALPHA * ht + acc + plet, name="ffn_out")
    return layer_norm(pre2, w["ln2_g"], w["ln2_b"], tm=tr)


def _moba_layer(x32, x16, w, rel_bias, *, M, prompt, B, S=None, k_pool=None, v_pool=None, layer=None,
                page_table=None):
    tm, _, _ = _tiles(M)
    D = x32.shape[1]
    q, k, v = [matmul(x16, [(w["qkv"], i * D)], n=D, tm=tm, tn=TN_WIDE, name="moba_qkv") for i in range(3)]
    if prompt:
        att = moba_prompt(q, k, v, rel_bias, B=B, S=S, H=A_HEADS)
    else:
        att = moba_sample(q[:B], k[:B], v[:B], k_pool, v_pool, layer, page_table, rel_bias, H=A_HEADS)
        att = jnp.pad(att, ((0, M - B), (0, 0)))
    pre1 = matmul(att, [(w["o"], 0)], n=D, tm=tm, tn=512, extras=(x32,),
                  epi=lambda acc, xt: ALPHA * xt + acc, name="moba_o")
    return pre1, k, v


def _ret_layer(x32, x16, w, s0, pos0, *, M, B, L):
    tm, tm_small, _ = _tiles(M)
    D = x32.shape[1]
    dk = D // RET_HEADS
    qkvg = matmul(x16, [(w["in"], 0)], n=6 * D, tm=tm, tn=TN_WIDE, name="ret_in")
    if L % RET_CHUNK == 0:
        C, c_real, Lp = RET_CHUNK, RET_CHUNK, L
        cos_t, sin_t = _rotation_tables(pos0 + jnp.arange(L, dtype=jnp.int32), dk)
        y, s_new = retention(qkvg, s0, w["gn_g"], cos_t, sin_t, B=B, L=L, H=RET_HEADS, C=C, c_real=c_real)
    else:
        C, c_real, Lp = SAMPLE_ROWS, L, SAMPLE_ROWS
        assert L == 1
        cos_t, sin_t = _rotation_tables(pos0 + jnp.arange(Lp, dtype=jnp.int32), dk)
        padded = jnp.pad(qkvg[:B].reshape(B, L, 6 * D), ((0, 0), (0, Lp - L), (0, 0))).reshape(B * Lp, 6 * D)
        y, s_new = retention(padded, s0, w["gn_g"], cos_t, sin_t, B=B, L=Lp, H=RET_HEADS, C=C, c_real=c_real)
        y = jnp.pad(y.reshape(B, Lp, 2 * D)[:, 0], ((0, M - B), (0, 0)))
    pre1 = matmul(y, [(w["o"], 0)], n=D, tm=tm_small, tn=512, extras=(x32,),
                  epi=lambda acc, xt: ALPHA * xt + acc, name="ret_o")
    return pre1, s_new


def _rwkv_layer(x32, x_prev, w, s0, *, M, B, L):
    tm, _, tr = _tiles(M)
    D = x32.shape[1]
    xr, xk, xv, xw, xa, xg = shift_mix(x32, x_prev, w["mu"], tm=min(tr, 128))
    rkv, lead = w["rkv"]
    r = matmul(xr, [((rkv, lead + (0,)), 0)], n=D, tm=tm, tn=TN_WIDE, name="rwkv_r")
    k = matmul(xk, [((rkv, lead + (1,)), 0)], n=D, tm=tm, tn=TN_WIDE, name="rwkv_k")
    v = matmul(xv, [((rkv, lead + (2,)), 0)], n=D, tm=tm, tn=TN_WIDE, name="rwkv_v")
    lora = _mat_shape(w["w1"])[1]
    wmid = matmul(xw, [(w["w1"], 0)], n=lora, tm=tm, tn=lora, epi=jnp.tanh, out_dtypes=(BF16,), name="rwkv_w1")

    def log_decay(acc, w0):
        z = -(w0 + acc)
        softplus = jnp.maximum(z, 0.0) + jnp.log1p(jnp.exp(-jnp.abs(z)))
        return -jnp.exp(-softplus - 0.5)

    lw = matmul(wmid, [(w["w2"], 0)], n=D, tm=tm, tn=512, extras=((w["w0"], 0),), epi=log_decay, name="rwkv_w2")
    amid = matmul(xa, [(w["a1"], 0)], n=lora, tm=tm, tn=lora, out_dtypes=(BF16,), name="rwkv_a1")
    a = matmul(amid, [(w["a2"], 0)], n=D, tm=tm, tn=512, extras=((w["a0"], 0),),
               epi=lambda acc, a0: jax.nn.sigmoid(a0 + acc), name="rwkv_a2")
    gl = _mat_shape(w["g1"])[1]
    gmid = matmul(xg, [(w["g1"], 0)], n=gl, tm=tm, tn=gl, epi=jax.nn.sigmoid, out_dtypes=(BF16,), name="rwkv_g1")
    g = matmul(gmid, [(w["g2"], 0)], n=D, tm=tm, tn=512, name="rwkv_g2")
    params = (w["k_k"], w["k_a"], w["r_k"], w["gn_g"], w["gn_b"])
    if L % RWKV_CHUNK == 0:
        y, s_new = rwkv_recurrence(r, k, v, lw, a, g, *params, s0, B=B, L=L, C=RWKV_CHUNK)
    else:
        assert L == 1
        Lp = SAMPLE_ROWS
        pad = lambda t: jnp.pad(t[:B].reshape(B, L, D), ((0, 0), (0, Lp - L), (0, 0))).reshape(B * Lp, D)
        y, s_new = rwkv_recurrence(*[pad(t) for t in (r, k, v, lw, a, g)], *params, s0, B=B, L=Lp, C=Lp)
        y = jnp.pad(y.reshape(B, Lp, D)[:, 0], ((0, M - B), (0, 0)))
    pre1 = matmul(y, [(w["o"], 0)], n=D, tm=tm, tn=512, extras=(x32,),
                  epi=lambda acc, xt: ALPHA * xt + acc, name="rwkv_o")
    return pre1, s_new


def kernel(x_prompt, x_sample, p_prompt, p_sample, cache_moba_k, cache_moba_v, page_table, state_ret, state_wkv, state_shift, ln1_g, ln1_b, ln2_g, ln2_b, w_ffn_in, w_ffn_out, w_ple_gate, w_ple_proj, rel_bias, a_w_qkv, a_w_o, r_w_in, r_gn_g, r_w_o, c_mu, c_w_rkv, c_w0, c_w1, c_w2, c_a0, c_a1, c_a2, c_g1, c_g2, c_k_k, c_k_a, c_r_k, c_gn_g, c_gn_b, c_w_o):
    Bp, Sp, D = x_prompt.shape
    Bs, Ts, _ = x_sample.shape
    assert Ts == 1 and Bs <= SAMPLE_ROWS
    Mp, Ms = Bp * Sp, SAMPLE_ROWS
    past = page_table.shape[1] * PAGE_SIZE
    bf = lambda t: t.astype(BF16)
    pad_rows = lambda t: jnp.pad(t, ((0, Ms - Bs), (0, 0)))

    xp32 = x_prompt.reshape(Mp, D)
    xs32 = pad_rows(x_sample.reshape(Bs, D))
    xp16, xs16 = bf(xp32), bf(xs32)
    k_pool, v_pool = cache_moba_k, cache_moba_v
    gpad = W_GATE_LORA_PAD - c_g1.shape[-1]
    w16 = dict(ffn_in=bf(w_ffn_in), ffn_out=bf(w_ffn_out), ple_gate=bf(w_ple_gate), ple_proj=bf(w_ple_proj),
               qkv=bf(a_w_qkv), a_o=bf(a_w_o), r_in=bf(r_w_in), r_o=bf(r_w_o), rkv=bf(c_w_rkv), w1=bf(c_w1),
               w2=bf(c_w2), a1=bf(c_a1), a2=bf(c_a2), g1=bf(jnp.pad(c_g1, ((0, 0), (0, 0), (0, gpad)))),
               g2=bf(jnp.pad(c_g2, ((0, 0), (0, gpad), (0, 0)))), c_o=bf(c_w_o))

    kp_l, vp_l, ks_l, vs_l, rp_l, rs_l, wp_l, ws_l, hp_l, hs_l = ([] for _ in range(10))
    for i in range(DEPTH):
        kind, j = i % 3, i // 3
        tail = dict(ln1_g=ln1_g[i], ln1_b=ln1_b[i], ln2_g=ln2_g[i], ln2_b=ln2_b[i],
                    ffn_in=(w16["ffn_in"], (i,)), ffn_out=(w16["ffn_out"], (i,)),
                    ple_gate=(w16["ple_gate"], (i,)), ple_proj=(w16["ple_proj"], (i,)))
        if kind == 0:
            w = dict(qkv=(w16["qkv"], (j,)), o=(w16["a_o"], (j,)))
            pre_p, kp, vp = _moba_layer(xp32, xp16, w, rel_bias, M=Mp, prompt=True, B=Bp, S=Sp)
            pre_s, ks, vs = _moba_layer(xs32, xs16, w, rel_bias, M=Ms, prompt=False, B=Bs, k_pool=k_pool,
                                        v_pool=v_pool, layer=j, page_table=page_table)
            kp_l.append(kp.reshape(Bp, Sp, A_HEADS, D // A_HEADS))
            vp_l.append(vp.reshape(Bp, Sp, A_HEADS, D // A_HEADS))
            ks_l.append(ks[:Bs].reshape(Bs, Ts, A_HEADS, D // A_HEADS))
            vs_l.append(vs[:Bs].reshape(Bs, Ts, A_HEADS, D // A_HEADS))
        elif kind == 1:
            w = dict(o=(w16["r_o"], (j,)), gn_g=r_gn_g[j])
            w["in"] = (w16["r_in"], (j,))
            s0 = jnp.zeros((Bp,) + state_ret.shape[2:], state_ret.dtype)
            pre_p, sp = _ret_layer(xp32, xp16, w, s0, 0, M=Mp, B=Bp, L=Sp)
            pre_s, ss = _ret_layer(xs32, xs16, w, state_ret[j], past, M=Ms, B=Bs, L=Ts)
            rp_l.append(sp)
            rs_l.append(ss)
        else:
            w = dict(mu=c_mu[j], w0=c_w0[j].reshape(1, D), a0=c_a0[j].reshape(1, D),
                     k_k=c_k_k[j], k_a=c_k_a[j], r_k=c_r_k[j], gn_g=c_gn_g[j], gn_b=c_gn_b[j], o=(w16["c_o"], (j,)))
            w.update({name: (w16[name], (j,)) for name in ("rkv", "w1", "w2", "a1", "a2", "g1", "g2")})
            xp3 = xp32.reshape(Bp, Sp, D)
            xprev_p = jnp.concatenate([jnp.zeros((Bp, 1, D), F32), xp3[:, :-1]], axis=1).reshape(Mp, D)
            xprev_s = pad_rows(state_shift[j])
            wk0 = jnp.zeros((Bp,) + state_wkv.shape[2:], state_wkv.dtype)
            pre_p, wkp = _rwkv_layer(xp32, xprev_p, w, wk0, M=Mp, B=Bp, L=Sp)
            pre_s, wks = _rwkv_layer(xs32, xprev_s, w, state_wkv[j], M=Ms, B=Bs, L=Ts)
            wp_l.append(wkp)
            ws_l.append(wks)
            hp_l.append(xp3[:, -1])
            hs_l.append(xs32[:Bs])
        xp32, xp16 = _layer_tail(pre_p, bf(p_prompt[i].reshape(Mp, -1)), tail, M=Mp)
        xs32, xs16 = _layer_tail(pre_s, bf(pad_rows(p_sample[i].reshape(Bs, -1))), tail, M=Ms)
    return (xp32.reshape(Bp, Sp, D), xs32[:Bs].reshape(Bs, Ts, D),
            jnp.stack(kp_l), jnp.stack(vp_l), jnp.stack(ks_l), jnp.stack(vs_l),
            jnp.stack(rp_l), jnp.stack(rs_l), jnp.stack(wp_l), jnp.stack(ws_l),
            jnp.stack(hp_l), jnp.stack(hs_l))
```

```python
import functools
import math

import numpy as np
import jax
import jax.numpy as jnp
from jax import lax
from jax.experimental import pallas as pl
from jax.experimental.pallas import tpu as pltpu

F32 = jnp.float32
BF16 = jnp.bfloat16

DEPTH = 4
LN_EPS = 1e-5
ALPHA = (2 * DEPTH) ** 0.25
A_HEADS = 32
MOBA_BLOCK = 256
MOBA_TOPK = 3
PAGE_SIZE = 128
REL_BUCKETS = 32
REL_MAX_DIST = 128
RET_HEADS = 16
RET_CHUNK = 128
RET_GN_EPS = 1e-5
W_HEAD = 64
W_GN_EPS = 64e-5
W_GATE_LORA_PAD = 512
NEG_INF = -1e30

V7X_VMEM_LIMIT_BYTES = 56 * 1024 * 1024
LANES = 128
F32_SUBLANES = 8
CACHE_HEAD_GROUP = F32_SUBLANES

MOBA_HEADS_PER_STEP = 4
RET_HEADS_PER_STEP = 4
RWKV_CHUNK = 64
RWKV_HEADS_PER_STEP = 16
SAMPLE_ROWS = 16


def _t5_bucket_thresholds():
    n = np.arange(0, 4 * REL_MAX_DIST)
    max_exact = REL_BUCKETS // 2
    nf = np.maximum(n, 1).astype(np.float32)
    large = max_exact + (np.log(nf / np.float32(max_exact)) / np.float32(math.log(REL_MAX_DIST / max_exact))
                         * np.float32(REL_BUCKETS - max_exact)).astype(np.int32)
    bucket = np.where(n < max_exact, n, np.minimum(large, REL_BUCKETS - 1))
    assert sorted(set(bucket.tolist())) == list(range(REL_BUCKETS))
    return tuple(int(np.argmax(bucket >= b)) for b in range(REL_BUCKETS))


T5_THRESHOLDS = _t5_bucket_thresholds()


def _params(*sem):
    return pltpu.CompilerParams(dimension_semantics=sem, vmem_limit_bytes=V7X_VMEM_LIMIT_BYTES)


def _bias_from_distance(dist, table_ref, h):
    t = jnp.full(dist.shape, table_ref[0, h], F32)
    for b in range(1, REL_BUCKETS):
        t = jnp.where(dist >= T5_THRESHOLDS[b], table_ref[b, h], t)
    return t


def _mm_kernel(epi, n_w, n_ex, x_ref, *refs):
    w_refs = refs[:n_w]
    ex_refs = refs[n_w:n_w + n_ex]
    o_refs = refs[n_w + n_ex:]
    x = x_ref[...]
    accs = [jnp.dot(x, w[...], preferred_element_type=F32) for w in w_refs]
    outs = epi(*accs, *[e[...] for e in ex_refs])
    if not isinstance(outs, tuple):
        outs = (outs,)
    for o_ref, o in zip(o_refs, outs):
        o_ref[...] = o.astype(o_ref.dtype)


def matmul(x, ws, *, n, tm, tn, epi=None, extras=(), out_dtypes=(F32,), x_buffers=2, name="mm"):
    M, K = x.shape
    assert M % tm == 0 and n % tn == 0, (M, tm, n, tn)
    if epi is None:
        epi = lambda a: a
    in_specs = [pl.BlockSpec((tm, K), lambda i, j: (i, 0), pipeline_mode=pl.Buffered(x_buffers))]
    args = [x]
    for w, col0 in ws:
        w, lead = w if isinstance(w, tuple) else (w, ())
        assert w.shape[-2] == K and col0 % tn == 0 and w.ndim == len(lead) + 2
        in_specs.append(pl.BlockSpec((None,) * len(lead) + (K, tn),
                                     lambda i, j, cb=col0 // tn, lead=lead: lead + (0, j + cb)))
        args.append(w)
    for e in extras:
        if isinstance(e, tuple):
            row, col0 = e
            assert row.shape[0] == 1 and col0 % tn == 0
            in_specs.append(pl.BlockSpec((1, tn), lambda i, j, cb=col0 // tn: (0, j + cb)))
            args.append(row)
        else:
            assert e.shape == (M, n)
            in_specs.append(pl.BlockSpec((tm, tn), lambda i, j: (i, j)))
            args.append(e)
    out_shape = tuple(jax.ShapeDtypeStruct((M, n), dt) for dt in out_dtypes)
    out_specs = tuple(pl.BlockSpec((tm, tn), lambda i, j: (i, j)) for _ in out_dtypes)
    res = pl.pallas_call(
        functools.partial(_mm_kernel, epi, len(ws), len(extras)),
        grid=(M // tm, n // tn),
        in_specs=in_specs,
        out_specs=out_specs,
        out_shape=out_shape,
        compiler_params=_params("arbitrary", "arbitrary"),
        name=name,
    )(*args)
    return res if len(out_dtypes) > 1 else res[0]


def _ln_kernel(x_ref, g_ref, b_ref, o32_ref, o16_ref):
    x = x_ref[...]
    mu = jnp.mean(x, axis=-1, keepdims=True)
    xc = x - mu
    var = jnp.mean(xc * xc, axis=-1, keepdims=True)
    y = xc * lax.rsqrt(var + LN_EPS) * g_ref[...] + b_ref[...]
    o32_ref[...] = y
    o16_ref[...] = y.astype(BF16)


def layer_norm(x, g, b, *, tm):
    M, D = x.shape
    row = pl.BlockSpec((1, D), lambda i: (0, 0))
    tile = pl.BlockSpec((tm, D), lambda i: (i, 0))
    return pl.pallas_call(
        _ln_kernel,
        grid=(M // tm,),
        in_specs=[tile, row, row],
        out_specs=(tile, tile),
        out_shape=(jax.ShapeDtypeStruct((M, D), F32), jax.ShapeDtypeStruct((M, D), BF16)),
        compiler_params=_params("arbitrary"),
        name="layer_norm",
    )(x, g.reshape(1, D), b.reshape(1, D))


def _moba_prompt_kernel(nb, G, tab_ref, q_ref, k_ref, v_ref, o_ref, bown, bprev, kmean, vt, m_s, l_s, acc_s):
    hg = pl.program_id(0)
    b = pl.program_id(1)
    qb = pl.program_id(2)
    blk = MOBA_BLOCK
    n_cand = nb - 1
    k_sel_n = min(MOBA_TOPK, n_cand)
    hd = q_ref.shape[1] // G
    a_scale = hd ** -0.5
    h0 = hg * G
    key = lax.broadcasted_iota(jnp.int32, (blk, blk), 0)
    qry = lax.broadcasted_iota(jnp.int32, (blk, blk), 1)
    heads = lambda x: jnp.stack([x[:, g * hd:(g + 1) * hd] for g in range(G)])

    @pl.when((b == 0) & (qb == 0))
    def _():
        for g in range(G):
            bown[g] = _bias_from_distance(jnp.maximum(qry - key, 0), tab_ref, h0 + g)
            bprev[g] = _bias_from_distance(qry - key + blk, tab_ref, h0 + g)

    @pl.when(qb == 0)
    def _():
        kmean[...] = jnp.zeros(kmean.shape, F32)
        for n in range(n_cand):
            mean_n = jnp.mean(k_ref[n * blk:(n + 1) * blk, :], axis=0, keepdims=True)
            for g in range(G):
                kmean[g, n:n + 1, :] = mean_n[:, g * hd:(g + 1) * hd]
        for n in range(nb):
            for g in range(G):
                vt[n, g] = v_ref[n * blk:(n + 1) * blk, g * hd:(g + 1) * hd].T.astype(BF16)

    q = heads(q_ref[...])
    q16 = q.astype(BF16)
    dn_kq = (((2,), (2,)), ((0,), (0,)))
    dn_vp = (((2,), (1,)), ((0,), (0,)))

    start = pl.multiple_of(qb * blk, blk)
    k_own = heads(k_ref[pl.ds(start, blk), :]).astype(BF16)
    s = lax.dot_general(k_own, q16, dn_kq, preferred_element_type=F32) * a_scale + bown[...]
    s = jnp.where(qry >= key, s, NEG_INF)
    m0 = jnp.max(s, axis=1, keepdims=True)
    p = jnp.exp(s - m0)
    m_s[...] = m0
    l_s[...] = jnp.sum(p, axis=1, keepdims=True)
    acc_s[...] = lax.dot_general(vt[qb], p.astype(BF16), dn_vp, preferred_element_type=F32)

    if n_cand > 0:
        ncols = kmean.shape[1]
        gate = lax.dot_general(kmean[...], q, dn_kq, precision=lax.Precision.HIGHEST,
                               preferred_element_type=F32)
        cand = lax.broadcasted_iota(jnp.int32, (G, ncols, blk), 1)
        gate = jnp.where(cand < qb, gate, NEG_INF)
        rank = jnp.zeros((G, ncols, blk), F32)
        for m in range(n_cand):
            gm = gate[:, m:m + 1, :]
            ahead = (gm > gate) | ((gm == gate) & (cand > m))
            rank = rank + jnp.where(ahead, 1.0, 0.0)
        sel = jnp.where((rank < k_sel_n) & (cand < qb), 1.0, 0.0)
        gid = lax.broadcasted_iota(jnp.int32, (G, 1, 1), 0)
        far_bias = jnp.full((G, 1, 1), tab_ref[REL_BUCKETS - 1, h0], F32)
        for g in range(1, G):
            far_bias = jnp.where(gid == g, tab_ref[REL_BUCKETS - 1, h0 + g], far_bias)

        for n in range(n_cand):
            @pl.when(n < qb)
            def _(n=n):
                kn = heads(k_ref[n * blk:(n + 1) * blk, :]).astype(BF16)
                bias = jnp.where(n == qb - 1, bprev[...], far_bias)
                sn = lax.dot_general(kn, q16, dn_kq, preferred_element_type=F32) * a_scale + bias
                sn = jnp.where(sel[:, n:n + 1, :] > 0.5, sn, NEG_INF)
                m_old = m_s[...]
                m_new = jnp.maximum(m_old, jnp.max(sn, axis=1, keepdims=True))
                scale_old = jnp.exp(m_old - m_new)
                pn = jnp.exp(sn - m_new)
                l_s[...] = scale_old * l_s[...] + jnp.sum(pn, axis=1, keepdims=True)
                acc_s[...] = scale_old * acc_s[...] + lax.dot_general(vt[n], pn.astype(BF16), dn_vp,
                                                                      preferred_element_type=F32)
                m_s[...] = m_new

    out = acc_s[...] / l_s[...]
    for g in range(G):
        o_ref[:, g * hd:(g + 1) * hd] = out[g].T.astype(o_ref.dtype)


def moba_prompt(q, k, v, rel_bias, *, B, S, H):
    M, D = q.shape
    hd = D // H
    blk = MOBA_BLOCK
    G = math.gcd(H, MOBA_HEADS_PER_STEP)
    assert S % blk == 0 and M == B * S and T5_THRESHOLDS[-1] <= blk
    nb = S // blk
    ncols = max(8, -(-(nb - 1) // 8) * 8)
    qspec = pl.BlockSpec((blk, G * hd), lambda hg, b, qb: (b * nb + qb, hg))
    kvspec = pl.BlockSpec((S, G * hd), lambda hg, b, qb: (b, hg))
    return pl.pallas_call(
        functools.partial(_moba_prompt_kernel, nb, G),
        grid=(H // G, B, nb),
        in_specs=[pl.BlockSpec(memory_space=pltpu.SMEM), qspec, kvspec, kvspec],
        out_specs=qspec,
        out_shape=jax.ShapeDtypeStruct((M, D), BF16),
        scratch_shapes=[pltpu.VMEM((G, blk, blk), F32), pltpu.VMEM((G, blk, blk), F32),
                        pltpu.VMEM((G, ncols, hd), F32), pltpu.VMEM((nb, G, hd, blk), BF16),
                        pltpu.VMEM((G, 1, blk), F32), pltpu.VMEM((G, 1, blk), F32),
                        pltpu.VMEM((G, hd, blk), F32)],
        compiler_params=_params("arbitrary", "arbitrary", "arbitrary"),
        name="moba_prompt",
    )(rel_bias, q, k, v)


def _moba_select_kernel(n_blocks, pt_ref, pa_ref, pb_ref, q_ref, sel_ref, sums):
    j = pl.program_id(1)
    sums[j] = jnp.sum(pa_ref[0, 0], axis=0) + jnp.sum(pb_ref[0, 0], axis=0)

    @pl.when(j == n_blocks - 1)
    def _():
        prod = (sums[...] * (1.0 / MOBA_BLOCK)) * q_ref[...]
        gate = jnp.sum(prod, axis=-1, keepdims=True)
        blk_id = lax.broadcasted_iota(jnp.int32, gate.shape, 0)
        rank = jnp.zeros(gate.shape, F32)
        for m in range(n_blocks):
            gm = gate[m:m + 1]
            ahead = (gm > gate) | ((gm == gate) & (blk_id > m))
            rank = rank + jnp.where(ahead, 1.0, 0.0)
        for r in range(sel_ref.shape[1]):
            sel_ref[0, r] = jnp.sum(jnp.where(rank == float(r), blk_id, 0), axis=0)


def _moba_sample_attn_kernel(n_sel_pages, past, pt_ref, sel_ref, tab_ref, q_ref, kn_ref, vn_ref, *refs):
    k_refs = refs[:n_sel_pages]
    v_refs = refs[n_sel_pages:2 * n_sel_pages]
    o_ref = refs[2 * n_sel_pages]
    b = pl.program_id(0)
    h = pl.program_id(1)
    n_heads = pl.num_programs(1)
    hd = q_ref.shape[-1]
    a_scale = hd ** -0.5
    pages_per_block = MOBA_BLOCK // PAGE_SIZE
    pick = lambda ref: ref[0, 0, :, h % CACHE_HEAD_GROUP, :]
    q = q_ref[0]
    s_own = jnp.sum(q * kn_ref[0], axis=-1, keepdims=True) * a_scale + tab_ref[0, h]
    r = lax.broadcasted_iota(jnp.int32, (PAGE_SIZE, 1), 0)
    scores = []
    for t in range(n_sel_pages):
        blk = sel_ref[(b * n_heads + h) * MOBA_TOPK + t // pages_per_block]
        pos0 = blk * MOBA_BLOCK + (t % pages_per_block) * PAGE_SIZE
        dist = past - (pos0 + r)
        bias = _bias_from_distance(dist, tab_ref, h)
        scores.append(jnp.sum(pick(k_refs[t]) * q, axis=-1, keepdims=True) * a_scale + bias)
    m = s_own
    for s in scores:
        m = jnp.maximum(m, jnp.max(s, axis=0, keepdims=True))
    p_own = jnp.exp(s_own - m)
    l = p_own
    acc = p_own * vn_ref[0]
    for t, s in enumerate(scores):
        p = jnp.exp(s - m)
        l = l + jnp.sum(p, axis=0, keepdims=True)
        acc = acc + jnp.sum(p * pick(v_refs[t]), axis=0, keepdims=True)
    o_ref[0] = (acc / l).astype(o_ref.dtype)


def moba_sample(q, k_new, v_new, k_pool, v_pool, layer, page_table, rel_bias, *, H):
    rows, D = q.shape
    B, n_pages = page_table.shape
    assert rows == B
    hd = D // H
    past = n_pages * PAGE_SIZE
    ppb = MOBA_BLOCK // PAGE_SIZE
    assert past % MOBA_BLOCK == 0 and H % CACHE_HEAD_GROUP == 0
    n_blocks = past // MOBA_BLOCK
    assert n_blocks >= MOBA_TOPK
    pt = page_table.reshape(-1)
    q3 = q.reshape(rows, 1, D)

    page = lambda half: pl.BlockSpec(
        (1, 1, PAGE_SIZE, H, hd), lambda b, j, pt_ref: (layer, pt_ref[b * n_pages + ppb * j + half], 0, 0, 0))
    sel = pl.pallas_call(
        functools.partial(_moba_select_kernel, n_blocks),
        grid_spec=pltpu.PrefetchScalarGridSpec(
            num_scalar_prefetch=1,
            grid=(B, n_blocks),
            in_specs=[page(0), page(1), pl.BlockSpec((1, H, hd), lambda b, j, pt_ref: (b, 0, 0))],
            out_specs=pl.BlockSpec((1, MOBA_TOPK, H, 1), lambda b, j, pt_ref: (b, 0, 0, 0)),
            scratch_shapes=[pltpu.VMEM((n_blocks, H, hd), F32)]),
        out_shape=jax.ShapeDtypeStruct((B, MOBA_TOPK, H, 1), jnp.int32),
        compiler_params=_params("arbitrary", "arbitrary"),
        name="moba_sample_select",
    )(pt, k_pool, k_pool, q.reshape(B, H, hd))
    sel_flat = jnp.transpose(sel[..., 0], (0, 2, 1)).reshape(-1)

    n_sel_pages = MOBA_TOPK * ppb

    def sel_page(t):
        return pl.BlockSpec(
            (1, 1, PAGE_SIZE, CACHE_HEAD_GROUP, hd),
            lambda b, h, pt_ref, sel_ref: (
                layer, pt_ref[b * n_pages + ppb * sel_ref[(b * H + h) * MOBA_TOPK + t // ppb] + t % ppb], 0,
                h // CACHE_HEAD_GROUP, 0))

    tok = pl.BlockSpec((1, 1, hd), lambda b, h, pt_ref, sel_ref: (b, 0, h))
    out = pl.pallas_call(
        functools.partial(_moba_sample_attn_kernel, n_sel_pages, past),
        grid_spec=pltpu.PrefetchScalarGridSpec(
            num_scalar_prefetch=2,
            grid=(B, H),
            in_specs=[pl.BlockSpec(memory_space=pltpu.SMEM), tok, tok, tok]
                     + [sel_page(t) for t in range(n_sel_pages)] * 2,
            out_specs=tok),
        out_shape=jax.ShapeDtypeStruct((rows, 1, D), BF16),
        compiler_params=_params("arbitrary", "arbitrary"),
        name="moba_sample_attn",
    )(pt, sel_flat, rel_bias, q3, k_new.reshape(rows, 1, D), v_new.reshape(rows, 1, D),
      *([k_pool] * n_sel_pages), *([v_pool] * n_sel_pages))
    return out.reshape(rows, D)


def _retention_kernel(c_real, lg_ref, q_ref, k_ref, v_ref, g_ref, cos_ref, sin_ref, gn_ref, s0_ref,
                      y_ref, s_out_ref, s_acc):
    hg = pl.program_id(1)
    c = pl.program_id(2)
    n_chunks = pl.num_programs(2)
    C, dk = cos_ref.shape
    G = q_ref.shape[1] // dk
    dv = v_ref.shape[1] // G

    @pl.when(c == 0)
    def _():
        s_acc[...] = s0_ref[0]

    gid = lax.broadcasted_iota(jnp.int32, (G, 1, 1), 0)
    lg = jnp.full((G, 1, 1), lg_ref[hg * G], F32)
    for g in range(1, G):
        lg = jnp.where(gid == g, lg_ref[hg * G + g], lg)
    cos = cos_ref[...]
    sin = sin_ref[...]
    even = (lax.broadcasted_iota(jnp.int32, (C, LANES), 1) % 2) == 0
    heads = lambda x, w: jnp.stack([x[:, g * w:(g + 1) * w] for g in range(G)])

    def rotate(x):
        tiles = []
        for t in range(dk // LANES):
            xt = x[:, t * LANES:(t + 1) * LANES]
            tiles.append(jnp.where(even, pltpu.roll(xt, LANES - 1, 1), pltpu.roll(xt, 1, 1)))
        return x * cos + jnp.concatenate(tiles, axis=1) * sin

    q = jnp.stack([rotate(q_ref[:, g * dk:(g + 1) * dk]) for g in range(G)])
    k = jnp.stack([rotate(k_ref[:, g * dk:(g + 1) * dk]) for g in range(G)]) * (dk ** -0.5)
    v16 = heads(v_ref[...], dv).astype(BF16)
    idx = lax.broadcasted_iota(jnp.int32, (1, C, 1), 1).astype(F32)
    q_dec = jnp.exp((idx + 1.0) * lg)
    k_dec = jnp.exp((c_real - 1.0 - idx) * lg)
    ii = lax.broadcasted_iota(jnp.int32, (1, C, C), 1)
    jj = lax.broadcasted_iota(jnp.int32, (1, C, C), 2)
    diff = (ii - jj).astype(F32)
    dmask = jnp.where(diff >= 0, jnp.exp(jnp.maximum(diff, 0.0) * lg), 0.0)
    dg = lambda a, b, dn: lax.dot_general(a.astype(BF16), b.astype(BF16), dn, preferred_element_type=F32)
    qk = dg(q, k, _DN_NT) * dmask
    s_prev = s_acc[...]
    o = dg(qk, v16, _DN_NN) + dg(q * q_dec, s_prev, _DN_NN)
    kv = dg(k * k_dec, v16, _DN_TN)
    c_dec = jnp.exp(c_real * lg)
    s_new = s_prev * c_dec + kv
    s_acc[...] = s_new

    mu = jnp.mean(o, axis=-1, keepdims=True)
    oc = o - mu
    var = jnp.mean(oc * oc, axis=-1, keepdims=True)
    gate = heads(g_ref[...], dv)
    y = oc * lax.rsqrt(var + RET_GN_EPS) * heads(gn_ref[...], dv) * (gate * jax.nn.sigmoid(gate))
    for g in range(G):
        y_ref[:, g * dv:(g + 1) * dv] = y[g].astype(y_ref.dtype)

    @pl.when(c == n_chunks - 1)
    def _():
        s_out_ref[0] = s_new


def retention(qkvg, s0, gn_g, cos_t, sin_t, *, B, L, H, C, c_real):
    M, N6 = qkvg.shape
    D = N6 // 6
    dk, dv = D // H, 2 * D // H
    assert L % C == 0 and M == B * L
    nC = L // C
    G = math.gcd(H, RET_HEADS_PER_STEP)
    nG = H // G
    log_g = jnp.log1p(-jnp.exp2(-5.0 - jnp.arange(H, dtype=F32)))
    tokspec = lambda width, sec: pl.BlockSpec((C, G * width), lambda b, hg, c: (b * nC + c, sec * nG + hg))
    tabspec = pl.BlockSpec((C, dk), lambda b, hg, c: (c, 0))
    sspec = pl.BlockSpec((1, G, dk, dv), lambda b, hg, c: (b, hg, 0, 0))
    return pl.pallas_call(
        functools.partial(_retention_kernel, float(c_real)),
        grid=(B, nG, nC),
        in_specs=[pl.BlockSpec(memory_space=pltpu.SMEM),
                  tokspec(dk, 0), tokspec(dk, 1), tokspec(dv, 1), tokspec(dv, 2),
                  tabspec, tabspec, pl.BlockSpec((1, G * dv), lambda b, hg, c: (0, hg)), sspec],
        out_specs=(pl.BlockSpec((C, G * dv), lambda b, hg, c: (b * nC + c, hg)), sspec),
        out_shape=(jax.ShapeDtypeStruct((M, 2 * D), BF16), jax.ShapeDtypeStruct(s0.shape, F32)),
        scratch_shapes=[pltpu.VMEM((G, dk, dv), F32)],
        compiler_params=_params("arbitrary", "arbitrary", "arbitrary"),
        name="retention",
    )(log_g, qkvg, qkvg, qkvg, qkvg, cos_t, sin_t, gn_g.reshape(1, 2 * D), s0)


def _rotation_tables(pos, dk):
    inv = 1.0 / (10000.0 ** jnp.linspace(0.0, 1.0, dk // 2, dtype=F32))
    ang = pos.astype(F32)[:, None] * inv
    cos = jnp.repeat(jnp.cos(ang), 2, axis=1)
    sin = jnp.stack([-jnp.sin(ang), jnp.sin(ang)], axis=-1).reshape(pos.shape[0], dk)
    return cos, sin


def _shift_mix_kernel(x_ref, xp_ref, mu_ref, *o_refs):
    x = x_ref[...]
    xx = xp_ref[...] - x
    for i, o_ref in enumerate(o_refs):
        o_ref[...] = (x + xx * mu_ref[i:i + 1, :]).astype(o_ref.dtype)


def shift_mix(x, x_prev, mu, *, tm):
    M, D = x.shape
    n = mu.shape[0]
    tile = pl.BlockSpec((tm, D), lambda i: (i, 0))
    return pl.pallas_call(
        _shift_mix_kernel,
        grid=(M // tm,),
        in_specs=[tile, tile, pl.BlockSpec((n, D), lambda i: (0, 0))],
        out_specs=tuple(tile for _ in range(n)),
        out_shape=tuple(jax.ShapeDtypeStruct((M, D), BF16) for _ in range(n)),
        compiler_params=_params("arbitrary"),
        name="rwkv_shift_mix",
    )(x, x_prev, mu)


def _split3(x):
    hi = x.astype(BF16)
    r1 = x - hi.astype(F32)
    mid = r1.astype(BF16)
    lo = (r1 - mid.astype(F32)).astype(BF16)
    return hi, mid, lo


def _dot16(a, b, dn):
    return lax.dot_general(a.astype(BF16), b.astype(BF16), dimension_numbers=dn, preferred_element_type=F32)


_DN_NN = (((2,), (1,)), ((0,), (0,)))
_DN_NT = (((2,), (2,)), ((0,), (0,)))
_DN_TN = (((1,), (1,)), ((0,), (0,)))


def _rwkv_kernel(r_ref, k_ref, v_ref, lw_ref, a_ref, g_ref, kk_ref, ka_ref, rk_ref, gng_ref, gnb_ref, s0_ref,
                 y_ref, s_out_ref, s_acc):
    c = pl.program_id(2)
    n_chunks = pl.num_programs(2)
    C = r_ref.shape[0]
    N = W_HEAD
    G = r_ref.shape[1] // N

    @pl.when(c == 0)
    def _():
        s_acc[...] = s0_ref[0]

    ti = lax.broadcasted_iota(jnp.int32, (C, C), 0)
    tj = lax.broadcasted_iota(jnp.int32, (C, C), 1)
    strict = ti > tj
    incl = ti >= tj
    tri16 = jnp.broadcast_to(jnp.where(incl, 1.0, 0.0).astype(BF16), (G, C, C))
    n_double = max(1, (C - 1).bit_length())

    heads = lambda ref: jnp.stack([ref[:, hh * N:(hh + 1) * N] for hh in range(G)])
    r, k, v, lw, a = heads(r_ref), heads(k_ref), heads(v_ref), heads(lw_ref), heads(a_ref)
    kk = k * heads(kk_ref)
    kk = kk / jnp.maximum(jnp.sqrt(jnp.sum(kk * kk, axis=-1, keepdims=True)), 1e-12)
    kh = k * (1.0 + (a - 1.0) * heads(ka_ref))
    av = -kk
    bv = kk * a
    dg = functools.partial(lax.dot_general, dimension_numbers=_DN_NN, preferred_element_type=F32)
    hi, mid, lo = _split3(lw)
    cum = dg(tri16, hi) + dg(tri16, mid) + dg(tri16, lo)
    e_in = jnp.exp(cum)
    e_ex = jnp.exp(cum - lw)
    e_neg = jnp.exp(-cum)
    at = av * e_ex
    rt = r * e_in
    bt = bv * e_neg
    kt = kh * e_neg
    s0 = s_acc[...]
    lh = jnp.concatenate([at, rt], axis=1)
    m_b = _dot16(lh, bt, _DN_NT)
    m_k = _dot16(lh, kt, _DN_NT)
    z = _dot16(lh, s0, _DN_NT)
    a_ab = jnp.where(strict, m_b[:, :C], 0.0)
    a_ak = jnp.where(strict, m_k[:, :C], 0.0)
    a_rb = jnp.where(incl, m_b[:, C:], 0.0)
    a_rk = jnp.where(incl, m_k[:, C:], 0.0)
    u = z[:, :C] + _dot16(a_ak, v, _DN_NN)
    pw = a_ab
    for it in range(n_double):
        u = u + _dot16(pw, u, _DN_NN)
        if it + 1 < n_double:
            pw = _dot16(pw, pw, _DN_NN)
    y = z[:, C:] + _dot16(a_rb, u, _DN_NN) + _dot16(a_rk, v, _DN_NN)
    wc = e_in[:, C - 1:C, :]
    s_acc[...] = s0 * wc + _dot16(u, bt * wc, _DN_TN) + _dot16(v, kt * wc, _DN_TN)

    mu = jnp.mean(y, axis=-1, keepdims=True)
    yc = y - mu
    var = jnp.mean(yc * yc, axis=-1, keepdims=True)
    yn = yc * lax.rsqrt(var + W_GN_EPS)
    bonus = jnp.sum(r * kh * heads(rk_ref), axis=-1, keepdims=True) * v
    out = (yn * heads(gng_ref) + heads(gnb_ref) + bonus) * heads(g_ref)
    for hh in range(G):
        y_ref[:, hh * N:(hh + 1) * N] = out[hh].astype(y_ref.dtype)

    @pl.when(c == n_chunks - 1)
    def _():
        s_out_ref[0] = s_acc[...]


def rwkv_recurrence(r, k, v, lw, a, g, k_k, k_a, r_k, gn_g, gn_b, s0, *, B, L, C):
    M, D = r.shape
    N = W_HEAD
    H = D // N
    G = math.gcd(H, RWKV_HEADS_PER_STEP)
    assert L % C == 0 and M == B * L
    nC = L // C
    tok = pl.BlockSpec((C, G * N), lambda b, hg, c: (b * nC + c, hg))
    row = pl.BlockSpec((1, G * N), lambda b, hg, c: (0, hg))
    sspec = pl.BlockSpec((1, G, N, N), lambda b, hg, c: (b, hg, 0, 0))
    rows = [p.reshape(1, D) for p in (k_k, k_a, r_k, gn_g, gn_b)]
    return pl.pallas_call(
        _rwkv_kernel,
        grid=(B, H // G, nC),
        in_specs=[tok] * 6 + [row] * 5 + [sspec],
        out_specs=(tok, sspec),
        out_shape=(jax.ShapeDtypeStruct((M, D), BF16), jax.ShapeDtypeStruct(s0.shape, F32)),
        scratch_shapes=[pltpu.VMEM((G, N, N), F32)],
        compiler_params=_params("arbitrary", "arbitrary", "arbitrary"),
        name="rwkv_recurrence",
    )(r, k, v, lw, a, g, *rows, s0)


def _tiles(M):
    if M >= 1024:
        return 1024, 256
    return M, M


TN_WIDE = 1024
TN_FFN = 256


def _mat_shape(w):
    return (w[0] if isinstance(w, tuple) else w).shape[-2:]


def _layer_tail(pre1, p16, w, *, M):
    tm, tr = _tiles(M)
    F, D = _mat_shape(w["ffn_out"])
    h32, h16 = layer_norm(pre1, w["ln1_g"], w["ln1_b"], tm=tr)
    act = matmul(h16, [(w["ffn_in"], 0), (w["ffn_in"], F)], n=F, tm=min(M, 2 * tm), tn=TN_FFN,
                 epi=lambda gt, up: gt * jax.nn.sigmoid(gt) * up, out_dtypes=(BF16,), name="ffn_in")
    pp = matmul(p16, [(w["ple_proj"], 0)], n=D, tm=tm, tn=TN_WIDE, name="ple_proj")
    ple = matmul(h16, [(w["ple_gate"], 0)], n=D, tm=tm, tn=512, extras=(pp,),
                 epi=lambda acc, ppt: jax.nn.sigmoid(acc) * ppt, name="ple_gate")
    pre2 = matmul(act, [(w["ffn_out"], 0)], n=D, tm=tm, tn=TN_FFN, extras=(ple, h32), x_buffers=1,
                  epi=lambda acc, plet, ht: ALPHA * ht + acc + plet, name="ffn_out")
    return layer_norm(pre2, w["ln2_g"], w["ln2_b"], tm=tr)


def _moba_layer(x32, x16, w, rel_bias, *, M, prompt, B, S=None, k_pool=None, v_pool=None, layer=None,
                page_table=None):
    tm, _ = _tiles(M)
    D = x32.shape[1]
    q, k, v = [matmul(x16, [(w["qkv"], i * D)], n=D, tm=tm, tn=TN_WIDE, name="moba_qkv") for i in range(3)]
    if prompt:
        att = moba_prompt(q, k, v, rel_bias, B=B, S=S, H=A_HEADS)
    else:
        att = moba_sample(q[:B], k[:B], v[:B], k_pool, v_pool, layer, page_table, rel_bias, H=A_HEADS)
        att = jnp.pad(att, ((0, M - B), (0, 0)))
    pre1 = matmul(att, [(w["o"], 0)], n=D, tm=tm, tn=512, extras=(x32,),
                  epi=lambda acc, xt: ALPHA * xt + acc, name="moba_o")
    return pre1, k, v


def _ret_layer(x32, x16, w, s0, pos0, *, M, B, L):
    tm, _ = _tiles(M)
    D = x32.shape[1]
    dk = D // RET_HEADS
    qkvg = matmul(x16, [(w["in"], 0)], n=6 * D, tm=tm, tn=TN_WIDE, name="ret_in")
    if L % RET_CHUNK == 0:
        C, c_real, Lp = RET_CHUNK, RET_CHUNK, L
        cos_t, sin_t = _rotation_tables(pos0 + jnp.arange(L, dtype=jnp.int32), dk)
        y, s_new = retention(qkvg, s0, w["gn_g"], cos_t, sin_t, B=B, L=L, H=RET_HEADS, C=C, c_real=c_real)
    else:
        C, c_real, Lp = SAMPLE_ROWS, L, SAMPLE_ROWS
        assert L == 1
        cos_t, sin_t = _rotation_tables(pos0 + jnp.arange(Lp, dtype=jnp.int32), dk)
        padded = jnp.pad(qkvg[:B].reshape(B, L, 6 * D), ((0, 0), (0, Lp - L), (0, 0))).reshape(B * Lp, 6 * D)
        y, s_new = retention(padded, s0, w["gn_g"], cos_t, sin_t, B=B, L=Lp, H=RET_HEADS, C=C, c_real=c_real)
        y = jnp.pad(y.reshape(B, Lp, 2 * D)[:, 0], ((0, M - B), (0, 0)))
    pre1 = matmul(y, [(w["o"], 0)], n=D, tm=tm, tn=512, extras=(x32,), x_buffers=1,
                  epi=lambda acc, xt: ALPHA * xt + acc, name="ret_o")
    return pre1, s_new


def _rwkv_layer(x32, x_prev, w, s0, *, M, B, L):
    tm, tr = _tiles(M)
    D = x32.shape[1]
    xr, xk, xv, xw, xa, xg = shift_mix(x32, x_prev, w["mu"], tm=min(tr, 128))
    rkv, lead = w["rkv"]
    r = matmul(xr, [((rkv, lead + (0,)), 0)], n=D, tm=tm, tn=TN_WIDE, name="rwkv_r")
    k = matmul(xk, [((rkv, lead + (1,)), 0)], n=D, tm=tm, tn=TN_WIDE, name="rwkv_k")
    v = matmul(xv, [((rkv, lead + (2,)), 0)], n=D, tm=tm, tn=TN_WIDE, name="rwkv_v")
    lora = _mat_shape(w["w1"])[1]
    wmid = matmul(xw, [(w["w1"], 0)], n=lora, tm=tm, tn=lora, epi=jnp.tanh, out_dtypes=(BF16,), name="rwkv_w1")

    def log_decay(acc, w0):
        z = -(w0 + acc)
        softplus = jnp.maximum(z, 0.0) + jnp.log1p(jnp.exp(-jnp.abs(z)))
        return -jnp.exp(-softplus - 0.5)

    lw = matmul(wmid, [(w["w2"], 0)], n=D, tm=tm, tn=512, extras=((w["w0"], 0),), epi=log_decay, name="rwkv_w2")
    amid = matmul(xa, [(w["a1"], 0)], n=lora, tm=tm, tn=lora, out_dtypes=(BF16,), name="rwkv_a1")
    a = matmul(amid, [(w["a2"], 0)], n=D, tm=tm, tn=512, extras=((w["a0"], 0),),
               epi=lambda acc, a0: jax.nn.sigmoid(a0 + acc), name="rwkv_a2")
    gl = _mat_shape(w["g1"])[1]
    gmid = matmul(xg, [(w["g1"], 0)], n=gl, tm=tm, tn=gl, epi=jax.nn.sigmoid, out_dtypes=(BF16,), name="rwkv_g1")
    g = matmul(gmid, [(w["g2"], 0)], n=D, tm=tm, tn=512, name="rwkv_g2")
    params = (w["k_k"], w["k_a"], w["r_k"], w["gn_g"], w["gn_b"])
    if L % RWKV_CHUNK == 0:
        y, s_new = rwkv_recurrence(r, k, v, lw, a, g, *params, s0, B=B, L=L, C=RWKV_CHUNK)
    else:
        assert L == 1
        Lp = SAMPLE_ROWS
        pad = lambda t: jnp.pad(t[:B].reshape(B, L, D), ((0, 0), (0, Lp - L), (0, 0))).reshape(B * Lp, D)
        y, s_new = rwkv_recurrence(*[pad(t) for t in (r, k, v, lw, a, g)], *params, s0, B=B, L=Lp, C=Lp)
        y = jnp.pad(y.reshape(B, Lp, D)[:, 0], ((0, M - B), (0, 0)))
    pre1 = matmul(y, [(w["o"], 0)], n=D, tm=tm, tn=512, extras=(x32,),
                  epi=lambda acc, xt: ALPHA * xt + acc, name="rwkv_o")
    return pre1, s_new


def kernel(x_prompt, x_sample, p_prompt, p_sample, cache_moba_k, cache_moba_v, page_table, state_ret, state_wkv, state_shift, ln1_g, ln1_b, ln2_g, ln2_b, w_ffn_in, w_ffn_out, w_ple_gate, w_ple_proj, rel_bias, a_w_qkv, a_w_o, r_w_in, r_gn_g, r_w_o, c_mu, c_w_rkv, c_w0, c_w1, c_w2, c_a0, c_a1, c_a2, c_g1, c_g2, c_k_k, c_k_a, c_r_k, c_gn_g, c_gn_b, c_w_o):
    Bp, Sp, D = x_prompt.shape
    Bs, Ts, _ = x_sample.shape
    assert Ts == 1 and Bs <= SAMPLE_ROWS
    Mp, Ms = Bp * Sp, SAMPLE_ROWS
    past = page_table.shape[1] * PAGE_SIZE
    bf = lambda t: t.astype(BF16)
    pad_rows = lambda t: jnp.pad(t, ((0, Ms - Bs), (0, 0)))

    xp32 = x_prompt.reshape(Mp, D)
    xs32 = pad_rows(x_sample.reshape(Bs, D))
    xp16, xs16 = bf(xp32), bf(xs32)
    k_pool, v_pool = cache_moba_k, cache_moba_v
    gpad = W_GATE_LORA_PAD - c_g1.shape[-1]
    w16 = dict(ffn_in=bf(w_ffn_in), ffn_out=bf(w_ffn_out), ple_gate=bf(w_ple_gate), ple_proj=bf(w_ple_proj),
               qkv=bf(a_w_qkv), a_o=bf(a_w_o), r_in=bf(r_w_in), r_o=bf(r_w_o), rkv=bf(c_w_rkv), w1=bf(c_w1),
               w2=bf(c_w2), a1=bf(c_a1), a2=bf(c_a2), g1=bf(jnp.pad(c_g1, ((0, 0), (0, 0), (0, gpad)))),
               g2=bf(jnp.pad(c_g2, ((0, 0), (0, gpad), (0, 0)))), c_o=bf(c_w_o))

    kp_l, vp_l, ks_l, vs_l, rp_l, rs_l, wp_l, ws_l, hp_l, hs_l = ([] for _ in range(10))
    for i in range(DEPTH):
        kind, j = i % 3, i // 3
        tail = dict(ln1_g=ln1_g[i], ln1_b=ln1_b[i], ln2_g=ln2_g[i], ln2_b=ln2_b[i],
                    ffn_in=(w16["ffn_in"], (i,)), ffn_out=(w16["ffn_out"], (i,)),
                    ple_gate=(w16["ple_gate"], (i,)), ple_proj=(w16["ple_proj"], (i,)))
        if kind == 0:
            w = dict(qkv=(w16["qkv"], (j,)), o=(w16["a_o"], (j,)))
            pre_p, kp, vp = _moba_layer(xp32, xp16, w, rel_bias, M=Mp, prompt=True, B=Bp, S=Sp)
            pre_s, ks, vs = _moba_layer(xs32, xs16, w, rel_bias, M=Ms, prompt=False, B=Bs, k_pool=k_pool,
                                        v_pool=v_pool, layer=j, page_table=page_table)
            kp_l.append(kp.reshape(Bp, Sp, A_HEADS, D // A_HEADS))
            vp_l.append(vp.reshape(Bp, Sp, A_HEADS, D // A_HEADS))
            ks_l.append(ks[:Bs].reshape(Bs, Ts, A_HEADS, D // A_HEADS))
            vs_l.append(vs[:Bs].reshape(Bs, Ts, A_HEADS, D // A_HEADS))
        elif kind == 1:
            w = dict(o=(w16["r_o"], (j,)), gn_g=r_gn_g[j])
            w["in"] = (w16["r_in"], (j,))
            s0 = jnp.zeros((Bp,) + state_ret.shape[2:], state_ret.dtype)
            pre_p, sp = _ret_layer(xp32, xp16, w, s0, 0, M=Mp, B=Bp, L=Sp)
            pre_s, ss = _ret_layer(xs32, xs16, w, state_ret[j], past, M=Ms, B=Bs, L=Ts)
            rp_l.append(sp)
            rs_l.append(ss)
        else:
            w = dict(mu=c_mu[j], w0=c_w0[j].reshape(1, D), a0=c_a0[j].reshape(1, D),
                     k_k=c_k_k[j], k_a=c_k_a[j], r_k=c_r_k[j], gn_g=c_gn_g[j], gn_b=c_gn_b[j], o=(w16["c_o"], (j,)))
            w.update({name: (w16[name], (j,)) for name in ("rkv", "w1", "w2", "a1", "a2", "g1", "g2")})
            xp3 = xp32.reshape(Bp, Sp, D)
            xprev_p = jnp.concatenate([jnp.zeros((Bp, 1, D), F32), xp3[:, :-1]], axis=1).reshape(Mp, D)
            xprev_s = pad_rows(state_shift[j])
            wk0 = jnp.zeros((Bp,) + state_wkv.shape[2:], state_wkv.dtype)
            pre_p, wkp = _rwkv_layer(xp32, xprev_p, w, wk0, M=Mp, B=Bp, L=Sp)
            pre_s, wks = _rwkv_layer(xs32, xprev_s, w, state_wkv[j], M=Ms, B=Bs, L=Ts)
            wp_l.append(wkp)
            ws_l.append(wks)
            hp_l.append(xp3[:, -1])
            hs_l.append(xs32[:Bs])
        xp32, xp16 = _layer_tail(pre_p, bf(p_prompt[i].reshape(Mp, -1)), tail, M=Mp)
        xs32, xs16 = _layer_tail(pre_s, bf(pad_rows(p_sample[i].reshape(Bs, -1))), tail, M=Ms)
    return (xp32.reshape(Bp, Sp, D), xs32[:Bs].reshape(Bs, Ts, D),
            jnp.stack(kp_l), jnp.stack(vp_l), jnp.stack(ks_l), jnp.stack(vs_l),
            jnp.stack(rp_l), jnp.stack(rs_l), jnp.stack(wp_l), jnp.stack(ws_l),
            jnp.stack(hp_l), jnp.stack(hs_l))
```

```python
import functools
import math

import numpy as np
import jax
import jax.numpy as jnp
from jax import lax
from jax.experimental import pallas as pl
from jax.experimental.pallas import tpu as pltpu

F32 = jnp.float32
BF16 = jnp.bfloat16

DEPTH = 4
LN_EPS = 1e-5
ALPHA = (2 * DEPTH) ** 0.25
A_HEADS = 32
MOBA_BLOCK = 256
MOBA_TOPK = 3
PAGE_SIZE = 128
REL_BUCKETS = 32
REL_MAX_DIST = 128
RET_HEADS = 16
RET_CHUNK = 128
RET_GN_EPS = 1e-5
W_HEAD = 64
W_GN_EPS = 64e-5
W_GATE_LORA_PAD = 512
NEG_INF = -1e30

V7X_VMEM_LIMIT_BYTES = 56 * 1024 * 1024
LANES = 128
F32_SUBLANES = 8
BF16_SUBLANES = 16
CACHE_HEAD_GROUP = F32_SUBLANES

MOBA_HEADS_PER_STEP = 4
RET_HEADS_PER_STEP = 4
RWKV_CHUNK = 64
RWKV_HEADS_PER_STEP = 16
SAMPLE_ROWS = 16
SAMPLE_REGION = 128


def _t5_bucket_thresholds():
    n = np.arange(0, 4 * REL_MAX_DIST)
    max_exact = REL_BUCKETS // 2
    nf = np.maximum(n, 1).astype(np.float32)
    large = max_exact + (np.log(nf / np.float32(max_exact)) / np.float32(math.log(REL_MAX_DIST / max_exact))
                         * np.float32(REL_BUCKETS - max_exact)).astype(np.int32)
    bucket = np.where(n < max_exact, n, np.minimum(large, REL_BUCKETS - 1))
    assert sorted(set(bucket.tolist())) == list(range(REL_BUCKETS))
    return tuple(int(np.argmax(bucket >= b)) for b in range(REL_BUCKETS))


T5_THRESHOLDS = _t5_bucket_thresholds()


def _params(*sem):
    return pltpu.CompilerParams(dimension_semantics=sem, vmem_limit_bytes=V7X_VMEM_LIMIT_BYTES)


def _bias_from_distance(dist, table_ref, h):
    t = jnp.full(dist.shape, table_ref[0, h], F32)
    for b in range(1, REL_BUCKETS):
        t = jnp.where(dist >= T5_THRESHOLDS[b], table_ref[b, h], t)
    return t


def _mm_kernel(epi, n_w, n_ex, n_out, w_stationary, x_ref, *refs):
    w_refs = refs[:n_w]
    ex_refs = refs[n_w:n_w + n_ex]
    o_refs = refs[n_w + n_ex:n_w + n_ex + n_out]
    if w_stationary:
        w16 = refs[n_w + n_ex + n_out]

        @pl.when(pl.program_id(1) == 0)
        def _():
            for t, w in enumerate(w_refs):
                w16[t] = w[...].astype(BF16)

        rhs = [w16.at[t] for t in range(n_w)]
    else:
        rhs = w_refs
    x = x_ref[...]
    accs = [jnp.dot(x, w[...], preferred_element_type=F32) for w in rhs]
    outs = epi(*accs, *[e[...] for e in ex_refs])
    if not isinstance(outs, tuple):
        outs = (outs,)
    for o_ref, o in zip(o_refs, outs):
        o_ref[...] = o.astype(o_ref.dtype)


def matmul(x, ws, *, n, tm, tn, epi=None, extras=(), out_dtypes=(F32,), x_buffers=2, w_stationary=False,
           name="mm"):
    M, K = x.shape
    assert M % tm == 0 and n % tn == 0, (M, tm, n, tn)
    if epi is None:
        epi = lambda a: a
    ij = (lambda f: (lambda j, i: f(i, j))) if w_stationary else (lambda f: f)
    in_specs = [pl.BlockSpec((tm, K), ij(lambda i, j: (i, 0)), pipeline_mode=pl.Buffered(x_buffers))]
    args = [x]
    for w, col0 in ws:
        w, lead = w if isinstance(w, tuple) else (w, ())
        assert w.shape[-2] == K and col0 % tn == 0 and w.ndim == len(lead) + 2
        assert w.dtype == (F32 if w_stationary else BF16)
        in_specs.append(pl.BlockSpec((None,) * len(lead) + (K, tn),
                                     ij(lambda i, j, cb=col0 // tn, lead=lead: lead + (0, j + cb))))
        args.append(w)
    for e in extras:
        if isinstance(e, tuple):
            row, col0 = e
            assert row.shape[0] == 1 and col0 % tn == 0
            in_specs.append(pl.BlockSpec((1, tn), ij(lambda i, j, cb=col0 // tn: (0, j + cb))))
            args.append(row)
        else:
            assert e.shape == (M, n)
            in_specs.append(pl.BlockSpec((tm, tn), ij(lambda i, j: (i, j))))
            args.append(e)
    out_shape = tuple(jax.ShapeDtypeStruct((M, n), dt) for dt in out_dtypes)
    out_specs = tuple(pl.BlockSpec((tm, tn), ij(lambda i, j: (i, j))) for _ in out_dtypes)
    res = pl.pallas_call(
        functools.partial(_mm_kernel, epi, len(ws), len(extras), len(out_dtypes), w_stationary),
        grid=(n // tn, M // tm) if w_stationary else (M // tm, n // tn),
        in_specs=in_specs,
        out_specs=out_specs,
        out_shape=out_shape,
        scratch_shapes=[pltpu.VMEM((len(ws), K, tn), BF16)] if w_stationary else [],
        compiler_params=_params("arbitrary", "arbitrary"),
        name=name,
    )(*args)
    return res if len(out_dtypes) > 1 else res[0]


def _ln_kernel(x_ref, g_ref, b_ref, o32_ref, o16_ref):
    x = x_ref[...]
    mu = jnp.mean(x, axis=-1, keepdims=True)
    xc = x - mu
    var = jnp.mean(xc * xc, axis=-1, keepdims=True)
    y = xc * lax.rsqrt(var + LN_EPS) * g_ref[...] + b_ref[...]
    o32_ref[...] = y
    o16_ref[...] = y.astype(BF16)


def layer_norm(x, g, b, *, tm):
    M, D = x.shape
    row = pl.BlockSpec((1, D), lambda i: (0, 0))
    tile = pl.BlockSpec((tm, D), lambda i: (i, 0))
    return pl.pallas_call(
        _ln_kernel,
        grid=(M // tm,),
        in_specs=[tile, row, row],
        out_specs=(tile, tile),
        out_shape=(jax.ShapeDtypeStruct((M, D), F32), jax.ShapeDtypeStruct((M, D), BF16)),
        compiler_params=_params("arbitrary"),
        name="layer_norm",
    )(x, g.reshape(1, D), b.reshape(1, D))


def _moba_prompt_kernel(nb, G, tab_ref, q_ref, k_ref, v_ref, o_ref, bown, bprev, kmean, vt, m_s, l_s, acc_s):
    hg = pl.program_id(0)
    b = pl.program_id(1)
    qb = pl.program_id(2)
    blk = MOBA_BLOCK
    n_cand = nb - 1
    k_sel_n = min(MOBA_TOPK, n_cand)
    hd = q_ref.shape[1] // G
    a_scale = hd ** -0.5
    h0 = hg * G
    key = lax.broadcasted_iota(jnp.int32, (blk, blk), 0)
    qry = lax.broadcasted_iota(jnp.int32, (blk, blk), 1)
    heads = lambda x: jnp.stack([x[:, g * hd:(g + 1) * hd] for g in range(G)])

    @pl.when((b == 0) & (qb == 0))
    def _():
        for g in range(G):
            bown[g] = _bias_from_distance(jnp.maximum(qry - key, 0), tab_ref, h0 + g)
            bprev[g] = _bias_from_distance(qry - key + blk, tab_ref, h0 + g)

    @pl.when(qb == 0)
    def _():
        kmean[...] = jnp.zeros(kmean.shape, F32)
        for n in range(n_cand):
            mean_n = jnp.mean(k_ref[n * blk:(n + 1) * blk, :], axis=0, keepdims=True)
            for g in range(G):
                kmean[g, n:n + 1, :] = mean_n[:, g * hd:(g + 1) * hd]
        for n in range(nb):
            for g in range(G):
                vt[n, g] = v_ref[n * blk:(n + 1) * blk, g * hd:(g + 1) * hd].T.astype(BF16)

    q = heads(q_ref[...])
    q16 = q.astype(BF16)
    dn_kq = (((2,), (2,)), ((0,), (0,)))
    dn_vp = (((2,), (1,)), ((0,), (0,)))

    start = pl.multiple_of(qb * blk, blk)
    k_own = heads(k_ref[pl.ds(start, blk), :]).astype(BF16)
    s = lax.dot_general(k_own, q16, dn_kq, preferred_element_type=F32) * a_scale + bown[...]
    s = jnp.where(qry >= key, s, NEG_INF)
    m0 = jnp.max(s, axis=1, keepdims=True)
    p = jnp.exp(s - m0)
    m_s[...] = m0
    l_s[...] = jnp.sum(p, axis=1, keepdims=True)
    acc_s[...] = lax.dot_general(vt[qb], p.astype(BF16), dn_vp, preferred_element_type=F32)

    if n_cand > 0:
        ncols = kmean.shape[1]
        gate = lax.dot_general(kmean[...], q, dn_kq, precision=lax.Precision.HIGHEST,
                               preferred_element_type=F32)
        cand = lax.broadcasted_iota(jnp.int32, (G, ncols, blk), 1)
        gate = jnp.where(cand < qb, gate, NEG_INF)
        rank = jnp.zeros((G, ncols, blk), F32)
        for m in range(n_cand):
            gm = gate[:, m:m + 1, :]
            ahead = (gm > gate) | ((gm == gate) & (cand > m))
            rank = rank + jnp.where(ahead, 1.0, 0.0)
        sel = jnp.where((rank < k_sel_n) & (cand < qb), 1.0, 0.0)
        gid = lax.broadcasted_iota(jnp.int32, (G, 1, 1), 0)
        far_bias = jnp.full((G, 1, 1), tab_ref[REL_BUCKETS - 1, h0], F32)
        for g in range(1, G):
            far_bias = jnp.where(gid == g, tab_ref[REL_BUCKETS - 1, h0 + g], far_bias)

        for n in range(n_cand):
            @pl.when(n < qb)
            def _(n=n):
                kn = heads(k_ref[n * blk:(n + 1) * blk, :]).astype(BF16)
                bias = jnp.where(n == qb - 1, bprev[...], far_bias)
                sn = lax.dot_general(kn, q16, dn_kq, preferred_element_type=F32) * a_scale + bias
                sn = jnp.where(sel[:, n:n + 1, :] > 0.5, sn, NEG_INF)
                m_old = m_s[...]
                m_new = jnp.maximum(m_old, jnp.max(sn, axis=1, keepdims=True))
                scale_old = jnp.exp(m_old - m_new)
                pn = jnp.exp(sn - m_new)
                l_s[...] = scale_old * l_s[...] + jnp.sum(pn, axis=1, keepdims=True)
                acc_s[...] = scale_old * acc_s[...] + lax.dot_general(vt[n], pn.astype(BF16), dn_vp,
                                                                      preferred_element_type=F32)
                m_s[...] = m_new

    out = acc_s[...] / l_s[...]
    for g in range(G):
        o_ref[:, g * hd:(g + 1) * hd] = out[g].T.astype(o_ref.dtype)


def moba_prompt(q, k, v, rel_bias, *, B, S, H):
    M, D = q.shape
    hd = D // H
    blk = MOBA_BLOCK
    G = math.gcd(H, MOBA_HEADS_PER_STEP)
    assert S % blk == 0 and M >= B * S and T5_THRESHOLDS[-1] <= blk
    nb = S // blk
    ncols = max(8, -(-(nb - 1) // 8) * 8)
    qspec = pl.BlockSpec((blk, G * hd), lambda hg, b, qb: (b * nb + qb, hg))
    kvspec = pl.BlockSpec((S, G * hd), lambda hg, b, qb: (b, hg))
    return pl.pallas_call(
        functools.partial(_moba_prompt_kernel, nb, G),
        grid=(H // G, B, nb),
        in_specs=[pl.BlockSpec(memory_space=pltpu.SMEM), qspec, kvspec, kvspec],
        out_specs=qspec,
        out_shape=jax.ShapeDtypeStruct((M, D), BF16),
        scratch_shapes=[pltpu.VMEM((G, blk, blk), F32), pltpu.VMEM((G, blk, blk), F32),
                        pltpu.VMEM((G, ncols, hd), F32), pltpu.VMEM((nb, G, hd, blk), BF16),
                        pltpu.VMEM((G, 1, blk), F32), pltpu.VMEM((G, 1, blk), F32),
                        pltpu.VMEM((G, hd, blk), F32)],
        compiler_params=_params("arbitrary", "arbitrary", "arbitrary"),
        name="moba_prompt",
    )(rel_bias, q, k, v)


def _moba_select_kernel(n_blocks, pt_ref, pa_ref, pb_ref, q_ref, sel_ref, sums):
    j = pl.program_id(1)
    sums[j] = jnp.sum(pa_ref[0, 0], axis=0) + jnp.sum(pb_ref[0, 0], axis=0)

    @pl.when(j == n_blocks - 1)
    def _():
        prod = (sums[...] * (1.0 / MOBA_BLOCK)) * q_ref[...]
        gate = jnp.sum(prod, axis=-1, keepdims=True)
        blk_id = lax.broadcasted_iota(jnp.int32, gate.shape, 0)
        rank = jnp.zeros(gate.shape, F32)
        for m in range(n_blocks):
            gm = gate[m:m + 1]
            ahead = (gm > gate) | ((gm == gate) & (blk_id > m))
            rank = rank + jnp.where(ahead, 1.0, 0.0)
        for r in range(sel_ref.shape[1]):
            sel_ref[0, r] = jnp.sum(jnp.where(rank == float(r), blk_id, 0), axis=0)


def _moba_sample_attn_kernel(n_sel_pages, past, pt_ref, sel_ref, tab_ref, q_ref, kn_ref, vn_ref, *refs):
    k_refs = refs[:n_sel_pages]
    v_refs = refs[n_sel_pages:2 * n_sel_pages]
    o_ref = refs[2 * n_sel_pages]
    b = pl.program_id(0)
    h = pl.program_id(1)
    n_heads = pl.num_programs(1)
    hd = q_ref.shape[-1]
    a_scale = hd ** -0.5
    pages_per_block = MOBA_BLOCK // PAGE_SIZE
    pick = lambda ref: ref[0, 0, :, h % CACHE_HEAD_GROUP, :]
    q = q_ref[0]
    s_own = jnp.sum(q * kn_ref[0], axis=-1, keepdims=True) * a_scale + tab_ref[0, h]
    r = lax.broadcasted_iota(jnp.int32, (PAGE_SIZE, 1), 0)
    scores = []
    for t in range(n_sel_pages):
        blk = sel_ref[(b * n_heads + h) * MOBA_TOPK + t // pages_per_block]
        pos0 = blk * MOBA_BLOCK + (t % pages_per_block) * PAGE_SIZE
        dist = past - (pos0 + r)
        bias = _bias_from_distance(dist, tab_ref, h)
        scores.append(jnp.sum(pick(k_refs[t]) * q, axis=-1, keepdims=True) * a_scale + bias)
    m = s_own
    for s in scores:
        m = jnp.maximum(m, jnp.max(s, axis=0, keepdims=True))
    p_own = jnp.exp(s_own - m)
    l = p_own
    acc = p_own * vn_ref[0]
    for t, s in enumerate(scores):
        p = jnp.exp(s - m)
        l = l + jnp.sum(p, axis=0, keepdims=True)
        acc = acc + jnp.sum(p * pick(v_refs[t]), axis=0, keepdims=True)
    o_ref[0] = (acc / l).astype(o_ref.dtype)


def moba_sample(q, k_new, v_new, k_pool, v_pool, layer, page_table, rel_bias, *, H):
    rows, D = q.shape
    B, n_pages = page_table.shape
    assert rows == B
    hd = D // H
    past = n_pages * PAGE_SIZE
    ppb = MOBA_BLOCK // PAGE_SIZE
    assert past % MOBA_BLOCK == 0 and H % CACHE_HEAD_GROUP == 0
    n_blocks = past // MOBA_BLOCK
    assert n_blocks >= MOBA_TOPK
    pt = page_table.reshape(-1)
    q3 = q.reshape(rows, 1, D)

    page = lambda half: pl.BlockSpec(
        (1, 1, PAGE_SIZE, H, hd), lambda b, j, pt_ref: (layer, pt_ref[b * n_pages + ppb * j + half], 0, 0, 0))
    sel = pl.pallas_call(
        functools.partial(_moba_select_kernel, n_blocks),
        grid_spec=pltpu.PrefetchScalarGridSpec(
            num_scalar_prefetch=1,
            grid=(B, n_blocks),
            in_specs=[page(0), page(1), pl.BlockSpec((1, H, hd), lambda b, j, pt_ref: (b, 0, 0))],
            out_specs=pl.BlockSpec((1, MOBA_TOPK, H, 1), lambda b, j, pt_ref: (b, 0, 0, 0)),
            scratch_shapes=[pltpu.VMEM((n_blocks, H, hd), F32)]),
        out_shape=jax.ShapeDtypeStruct((B, MOBA_TOPK, H, 1), jnp.int32),
        compiler_params=_params("arbitrary", "arbitrary"),
        name="moba_sample_select",
    )(pt, k_pool, k_pool, q.reshape(B, H, hd))
    sel_flat = jnp.transpose(sel[..., 0], (0, 2, 1)).reshape(-1)

    n_sel_pages = MOBA_TOPK * ppb

    def sel_page(t):
        return pl.BlockSpec(
            (1, 1, PAGE_SIZE, CACHE_HEAD_GROUP, hd),
            lambda b, h, pt_ref, sel_ref: (
                layer, pt_ref[b * n_pages + ppb * sel_ref[(b * H + h) * MOBA_TOPK + t // ppb] + t % ppb], 0,
                h // CACHE_HEAD_GROUP, 0))

    tok = pl.BlockSpec((1, 1, hd), lambda b, h, pt_ref, sel_ref: (b, 0, h))
    out = pl.pallas_call(
        functools.partial(_moba_sample_attn_kernel, n_sel_pages, past),
        grid_spec=pltpu.PrefetchScalarGridSpec(
            num_scalar_prefetch=2,
            grid=(B, H),
            in_specs=[pl.BlockSpec(memory_space=pltpu.SMEM), tok, tok, tok]
                     + [sel_page(t) for t in range(n_sel_pages)] * 2,
            out_specs=tok),
        out_shape=jax.ShapeDtypeStruct((rows, 1, D), BF16),
        compiler_params=_params("arbitrary", "arbitrary"),
        name="moba_sample_attn",
    )(pt, sel_flat, rel_bias, q3, k_new.reshape(rows, 1, D), v_new.reshape(rows, 1, D),
      *([k_pool] * n_sel_pages), *([v_pool] * n_sel_pages))
    return out.reshape(rows, D)


def _retention_kernel(c_real, lg_ref, q_ref, k_ref, v_ref, g_ref, cos_ref, sin_ref, gn_ref, s0_ref,
                      y_ref, s_out_ref, s_acc):
    hg = pl.program_id(1)
    c = pl.program_id(2)
    n_chunks = pl.num_programs(2)
    C, dk = cos_ref.shape
    G = q_ref.shape[1] // dk
    dv = v_ref.shape[1] // G

    @pl.when(c == 0)
    def _():
        s_acc[...] = s0_ref[0]

    gid = lax.broadcasted_iota(jnp.int32, (G, 1, 1), 0)
    lg = jnp.full((G, 1, 1), lg_ref[hg * G], F32)
    for g in range(1, G):
        lg = jnp.where(gid == g, lg_ref[hg * G + g], lg)
    cos = cos_ref[...]
    sin = sin_ref[...]
    even = (lax.broadcasted_iota(jnp.int32, (C, LANES), 1) % 2) == 0
    heads = lambda x, w: jnp.stack([x[:, g * w:(g + 1) * w] for g in range(G)])

    def rotate(x):
        tiles = []
        for t in range(dk // LANES):
            xt = x[:, t * LANES:(t + 1) * LANES]
            tiles.append(jnp.where(even, pltpu.roll(xt, LANES - 1, 1), pltpu.roll(xt, 1, 1)))
        return x * cos + jnp.concatenate(tiles, axis=1) * sin

    q = jnp.stack([rotate(q_ref[:, g * dk:(g + 1) * dk]) for g in range(G)])
    k = jnp.stack([rotate(k_ref[:, g * dk:(g + 1) * dk]) for g in range(G)]) * (dk ** -0.5)
    v16 = heads(v_ref[...], dv).astype(BF16)
    idx = lax.broadcasted_iota(jnp.int32, (1, C, 1), 1).astype(F32)
    q_dec = jnp.exp((idx + 1.0) * lg)
    k_dec = jnp.exp((c_real - 1.0 - idx) * lg)
    ii = lax.broadcasted_iota(jnp.int32, (1, C, C), 1)
    jj = lax.broadcasted_iota(jnp.int32, (1, C, C), 2)
    diff = (ii - jj).astype(F32)
    dmask = jnp.where(diff >= 0, jnp.exp(jnp.maximum(diff, 0.0) * lg), 0.0)
    dg = lambda a, b, dn: lax.dot_general(a.astype(BF16), b.astype(BF16), dn, preferred_element_type=F32)
    qk = dg(q, k, _DN_NT) * dmask
    s_prev = s_acc[...]
    o = dg(qk, v16, _DN_NN) + dg(q * q_dec, s_prev, _DN_NN)
    kv = dg(k * k_dec, v16, _DN_TN)
    c_dec = jnp.exp(c_real * lg)
    s_new = s_prev * c_dec + kv
    s_acc[...] = s_new

    mu = jnp.mean(o, axis=-1, keepdims=True)
    oc = o - mu
    var = jnp.mean(oc * oc, axis=-1, keepdims=True)
    gate = heads(g_ref[...], dv)
    y = oc * lax.rsqrt(var + RET_GN_EPS) * heads(gn_ref[...], dv) * (gate * jax.nn.sigmoid(gate))
    for g in range(G):
        y_ref[:, g * dv:(g + 1) * dv] = y[g].astype(y_ref.dtype)

    @pl.when(c == n_chunks - 1)
    def _():
        s_out_ref[0] = s_new


def retention(qkvg, s0, gn_g, cos_t, sin_t, *, B, L, H, C, c_real):
    M, N6 = qkvg.shape
    D = N6 // 6
    dk, dv = D // H, 2 * D // H
    assert L % C == 0 and M >= B * L
    nC = L // C
    G = math.gcd(H, RET_HEADS_PER_STEP)
    nG = H // G
    log_g = jnp.log1p(-jnp.exp2(-5.0 - jnp.arange(H, dtype=F32)))
    tokspec = lambda width, sec: pl.BlockSpec((C, G * width), lambda b, hg, c: (b * nC + c, sec * nG + hg))
    tabspec = pl.BlockSpec((C, dk), lambda b, hg, c: (c, 0))
    sspec = pl.BlockSpec((1, G, dk, dv), lambda b, hg, c: (b, hg, 0, 0))
    return pl.pallas_call(
        functools.partial(_retention_kernel, float(c_real)),
        grid=(B, nG, nC),
        in_specs=[pl.BlockSpec(memory_space=pltpu.SMEM),
                  tokspec(dk, 0), tokspec(dk, 1), tokspec(dv, 1), tokspec(dv, 2),
                  tabspec, tabspec, pl.BlockSpec((1, G * dv), lambda b, hg, c: (0, hg)), sspec],
        out_specs=(pl.BlockSpec((C, G * dv), lambda b, hg, c: (b * nC + c, hg)), sspec),
        out_shape=(jax.ShapeDtypeStruct((M, 2 * D), BF16), jax.ShapeDtypeStruct(s0.shape, F32)),
        scratch_shapes=[pltpu.VMEM((G, dk, dv), F32)],
        compiler_params=_params("arbitrary", "arbitrary", "arbitrary"),
        name="retention",
    )(log_g, qkvg, qkvg, qkvg, qkvg, cos_t, sin_t, gn_g.reshape(1, 2 * D), s0)


def _rotation_tables(pos, dk):
    inv = 1.0 / (10000.0 ** jnp.linspace(0.0, 1.0, dk // 2, dtype=F32))
    ang = pos.astype(F32)[:, None] * inv
    cos = jnp.repeat(jnp.cos(ang), 2, axis=1)
    sin = jnp.stack([-jnp.sin(ang), jnp.sin(ang)], axis=-1).reshape(pos.shape[0], dk)
    return cos, sin


def _shift_mix_kernel(x_ref, xp_ref, mu_ref, *o_refs):
    x = x_ref[...]
    xx = xp_ref[...] - x
    for i, o_ref in enumerate(o_refs):
        o_ref[...] = (x + xx * mu_ref[i:i + 1, :]).astype(o_ref.dtype)


def shift_mix(x, x_prev, mu, *, tm):
    M, D = x.shape
    n = mu.shape[0]
    tile = pl.BlockSpec((tm, D), lambda i: (i, 0))
    return pl.pallas_call(
        _shift_mix_kernel,
        grid=(M // tm,),
        in_specs=[tile, tile, pl.BlockSpec((n, D), lambda i: (0, 0))],
        out_specs=tuple(tile for _ in range(n)),
        out_shape=tuple(jax.ShapeDtypeStruct((M, D), BF16) for _ in range(n)),
        compiler_params=_params("arbitrary"),
        name="rwkv_shift_mix",
    )(x, x_prev, mu)


def _split3(x):
    hi = x.astype(BF16)
    r1 = x - hi.astype(F32)
    mid = r1.astype(BF16)
    lo = (r1 - mid.astype(F32)).astype(BF16)
    return hi, mid, lo


def _dot16(a, b, dn):
    return lax.dot_general(a.astype(BF16), b.astype(BF16), dimension_numbers=dn, preferred_element_type=F32)


_DN_NN = (((2,), (1,)), ((0,), (0,)))
_DN_NT = (((2,), (2,)), ((0,), (0,)))
_DN_TN = (((1,), (1,)), ((0,), (0,)))


def _rwkv_kernel(r_ref, k_ref, v_ref, lw_ref, a_ref, g_ref, kk_ref, ka_ref, rk_ref, gng_ref, gnb_ref, s0_ref,
                 y_ref, s_out_ref, s_acc):
    c = pl.program_id(2)
    n_chunks = pl.num_programs(2)
    C = r_ref.shape[0]
    N = W_HEAD
    G = r_ref.shape[1] // N

    @pl.when(c == 0)
    def _():
        s_acc[...] = s0_ref[0]

    ti = lax.broadcasted_iota(jnp.int32, (C, C), 0)
    tj = lax.broadcasted_iota(jnp.int32, (C, C), 1)
    strict = ti > tj
    incl = ti >= tj
    tri16 = jnp.broadcast_to(jnp.where(incl, 1.0, 0.0).astype(BF16), (G, C, C))
    n_double = max(1, (C - 1).bit_length())

    heads = lambda ref: jnp.stack([ref[:, hh * N:(hh + 1) * N] for hh in range(G)])
    r, k, v, lw, a = heads(r_ref), heads(k_ref), heads(v_ref), heads(lw_ref), heads(a_ref)
    kk = k * heads(kk_ref)
    kk = kk / jnp.maximum(jnp.sqrt(jnp.sum(kk * kk, axis=-1, keepdims=True)), 1e-12)
    kh = k * (1.0 + (a - 1.0) * heads(ka_ref))
    av = -kk
    bv = kk * a
    dg = functools.partial(lax.dot_general, dimension_numbers=_DN_NN, preferred_element_type=F32)
    hi, mid, lo = _split3(lw)
    cum = dg(tri16, hi) + dg(tri16, mid) + dg(tri16, lo)
    e_in = jnp.exp(cum)
    e_ex = jnp.exp(cum - lw)
    e_neg = jnp.exp(-cum)
    at = av * e_ex
    rt = r * e_in
    bt = bv * e_neg
    kt = kh * e_neg
    s0 = s_acc[...]
    lh = jnp.concatenate([at, rt], axis=1)
    m_b = _dot16(lh, bt, _DN_NT)
    m_k = _dot16(lh, kt, _DN_NT)
    z = _dot16(lh, s0, _DN_NT)
    a_ab = jnp.where(strict, m_b[:, :C], 0.0)
    a_ak = jnp.where(strict, m_k[:, :C], 0.0)
    a_rb = jnp.where(incl, m_b[:, C:], 0.0)
    a_rk = jnp.where(incl, m_k[:, C:], 0.0)
    u = z[:, :C] + _dot16(a_ak, v, _DN_NN)
    pw = a_ab
    for it in range(n_double):
        u = u + _dot16(pw, u, _DN_NN)
        if it + 1 < n_double:
            pw = _dot16(pw, pw, _DN_NN)
    y = z[:, C:] + _dot16(a_rb, u, _DN_NN) + _dot16(a_rk, v, _DN_NN)
    wc = e_in[:, C - 1:C, :]
    s_acc[...] = s0 * wc + _dot16(u, bt * wc, _DN_TN) + _dot16(v, kt * wc, _DN_TN)

    mu = jnp.mean(y, axis=-1, keepdims=True)
    yc = y - mu
    var = jnp.mean(yc * yc, axis=-1, keepdims=True)
    yn = yc * lax.rsqrt(var + W_GN_EPS)
    bonus = jnp.sum(r * kh * heads(rk_ref), axis=-1, keepdims=True) * v
    out = (yn * heads(gng_ref) + heads(gnb_ref) + bonus) * heads(g_ref)
    for hh in range(G):
        y_ref[:, hh * N:(hh + 1) * N] = out[hh].astype(y_ref.dtype)

    @pl.when(c == n_chunks - 1)
    def _():
        s_out_ref[0] = s_acc[...]


def rwkv_recurrence(r, k, v, lw, a, g, k_k, k_a, r_k, gn_g, gn_b, s0, *, B, L, C):
    M, D = r.shape
    N = W_HEAD
    H = D // N
    G = math.gcd(H, RWKV_HEADS_PER_STEP)
    assert L % C == 0 and M >= B * L
    nC = L // C
    tok = pl.BlockSpec((C, G * N), lambda b, hg, c: (b * nC + c, hg))
    row = pl.BlockSpec((1, G * N), lambda b, hg, c: (0, hg))
    sspec = pl.BlockSpec((1, G, N, N), lambda b, hg, c: (b, hg, 0, 0))
    rows = [p.reshape(1, D) for p in (k_k, k_a, r_k, gn_g, gn_b)]
    return pl.pallas_call(
        _rwkv_kernel,
        grid=(B, H // G, nC),
        in_specs=[tok] * 6 + [row] * 5 + [sspec],
        out_specs=(tok, sspec),
        out_shape=(jax.ShapeDtypeStruct((M, D), BF16), jax.ShapeDtypeStruct(s0.shape, F32)),
        scratch_shapes=[pltpu.VMEM((G, N, N), F32)],
        compiler_params=_params("arbitrary", "arbitrary", "arbitrary"),
        name="rwkv_recurrence",
    )(r, k, v, lw, a, g, *rows, s0)


def _row_tile(M, target):
    best = None
    for t in range(BF16_SUBLANES, min(M, target) + 1, BF16_SUBLANES):
        if M % t == 0:
            best = t
    assert best is not None, M
    return best


def _tiles(M):
    return _row_tile(M, MATMUL_ROWS), _row_tile(M, ROWWISE_ROWS)


MATMUL_ROWS = 1088
ROWWISE_ROWS = 320
TN_WIDE = 1024
TN_STAT = 512
TN_FFN = 256


def _with_sample_rows(full, sample_rows, Mp):
    pad = full.shape[0] - Mp - sample_rows.shape[0]
    block = jnp.pad(sample_rows.astype(full.dtype), ((0, pad), (0, 0)))
    return lax.dynamic_update_slice(full, block, (Mp, 0))


def _mat_shape(w):
    return (w[0] if isinstance(w, tuple) else w).shape[-2:]


def _layer_tail(pre1, p16, w):
    tm, tr = _tiles(pre1.shape[0])
    F, D = _mat_shape(w["ffn_out"])
    h32, h16 = layer_norm(pre1, w["ln1_g"], w["ln1_b"], tm=tr)
    act = matmul(h16, [(w["ffn_in"], 0), (w["ffn_in"], F)], n=F, tm=tm, tn=TN_FFN, w_stationary=True,
                 epi=lambda gt, up: gt * jax.nn.sigmoid(gt) * up, out_dtypes=(BF16,), name="ffn_in")
    pp = matmul(p16, [(w["ple_proj"], 0)], n=D, tm=tm, tn=TN_STAT, w_stationary=True, name="ple_proj")
    ple = matmul(h16, [(w["ple_gate"], 0)], n=D, tm=tm, tn=TN_STAT, extras=(pp,), w_stationary=True,
                 epi=lambda acc, ppt: jax.nn.sigmoid(acc) * ppt, name="ple_gate")
    pre2 = matmul(act, [(w["ffn_out"], 0)], n=D, tm=tm, tn=TN_FFN, extras=(ple, h32), x_buffers=1,
                  epi=lambda acc, plet, ht: ALPHA * ht + acc + plet, name="ffn_out")
    return layer_norm(pre2, w["ln2_g"], w["ln2_b"], tm=tr)


def _moba_layer(x32, x16, w, rel_bias, k_pool, v_pool, layer, page_table, *, Bp, Sp, Bs):
    tm, _ = _tiles(x32.shape[0])
    D = x32.shape[1]
    Mp = Bp * Sp
    q, k, v = [matmul(x16, [(w["qkv"], i * D)], n=D, tm=tm, tn=TN_STAT, w_stationary=True, name="moba_qkv")
               for i in range(3)]
    att = moba_prompt(q, k, v, rel_bias, B=Bp, S=Sp, H=A_HEADS)
    sample = lambda t: t[Mp:Mp + Bs]
    att_s = moba_sample(sample(q), sample(k), sample(v), k_pool, v_pool, layer, page_table, rel_bias, H=A_HEADS)
    att = _with_sample_rows(att, att_s, Mp)
    pre1 = matmul(att, [(w["o"], 0)], n=D, tm=tm, tn=TN_STAT, extras=(x32,), w_stationary=True,
                  epi=lambda acc, xt: ALPHA * xt + acc, name="moba_o")
    return pre1, k, v


def _ret_layer(x32, x16, w, s0_sample, past, *, Bp, Sp, Bs):
    tm, _ = _tiles(x32.shape[0])
    D = x32.shape[1]
    Mp = Bp * Sp
    dk = D // RET_HEADS
    qkvg = matmul(x16, [(w["in"], 0)], n=6 * D, tm=tm, tn=TN_STAT, w_stationary=True, name="ret_in")
    assert Sp % RET_CHUNK == 0
    cos_t, sin_t = _rotation_tables(jnp.arange(Sp, dtype=jnp.int32), dk)
    s0 = jnp.zeros((Bp,) + s0_sample.shape[1:], s0_sample.dtype)
    y, s_prompt = retention(qkvg, s0, w["gn_g"], cos_t, sin_t, B=Bp, L=Sp, H=RET_HEADS, C=RET_CHUNK,
                            c_real=RET_CHUNK)
    Lp = SAMPLE_ROWS
    cos_t, sin_t = _rotation_tables(past + jnp.arange(Lp, dtype=jnp.int32), dk)
    padded = jnp.pad(qkvg[Mp:Mp + Bs].reshape(Bs, 1, 6 * D), ((0, 0), (0, Lp - 1), (0, 0))).reshape(Bs * Lp, 6 * D)
    y_s, s_sample = retention(padded, s0_sample, w["gn_g"], cos_t, sin_t, B=Bs, L=Lp, H=RET_HEADS, C=Lp, c_real=1)
    y = _with_sample_rows(y, y_s.reshape(Bs, Lp, 2 * D)[:, 0], Mp)
    pre1 = matmul(y, [(w["o"], 0)], n=D, tm=tm, tn=512, extras=(x32,), x_buffers=1,
                  epi=lambda acc, xt: ALPHA * xt + acc, name="ret_o")
    return pre1, s_prompt, s_sample


def _rwkv_layer(x32, x_prev, w, s0_sample, *, Bp, Sp, Bs):
    M, D = x32.shape
    tm, tr = _tiles(M)
    Mp = Bp * Sp
    xr, xk, xv, xw, xa, xg = shift_mix(x32, x_prev, w["mu"], tm=_row_tile(M, LANES))
    rkv, lead = w["rkv"]
    mm = functools.partial(matmul, tm=tm, w_stationary=True)
    r = mm(xr, [((rkv, lead + (0,)), 0)], n=D, tn=TN_STAT, name="rwkv_r")
    k = mm(xk, [((rkv, lead + (1,)), 0)], n=D, tn=TN_STAT, name="rwkv_k")
    v = mm(xv, [((rkv, lead + (2,)), 0)], n=D, tn=TN_STAT, name="rwkv_v")
    lora = _mat_shape(w["w1"])[1]
    wmid = mm(xw, [(w["w1"], 0)], n=lora, tn=lora, epi=jnp.tanh, out_dtypes=(BF16,), name="rwkv_w1")

    def log_decay(acc, w0):
        z = -(w0 + acc)
        softplus = jnp.maximum(z, 0.0) + jnp.log1p(jnp.exp(-jnp.abs(z)))
        return -jnp.exp(-softplus - 0.5)

    lw = mm(wmid, [(w["w2"], 0)], n=D, tn=TN_STAT, extras=((w["w0"], 0),), epi=log_decay, name="rwkv_w2")
    amid = mm(xa, [(w["a1"], 0)], n=lora, tn=lora, out_dtypes=(BF16,), name="rwkv_a1")
    a = mm(amid, [(w["a2"], 0)], n=D, tn=TN_STAT, extras=((w["a0"], 0),),
           epi=lambda acc, a0: jax.nn.sigmoid(a0 + acc), name="rwkv_a2")
    gl = _mat_shape(w["g1"])[1]
    gmid = mm(xg, [(w["g1"], 0)], n=gl, tn=gl, epi=jax.nn.sigmoid, out_dtypes=(BF16,), name="rwkv_g1")
    g = mm(gmid, [(w["g2"], 0)], n=D, tn=TN_STAT, name="rwkv_g2")
    params = (w["k_k"], w["k_a"], w["r_k"], w["gn_g"], w["gn_b"])
    assert Sp % RWKV_CHUNK == 0
    s0 = jnp.zeros((Bp,) + s0_sample.shape[1:], s0_sample.dtype)
    y, s_prompt = rwkv_recurrence(r, k, v, lw, a, g, *params, s0, B=Bp, L=Sp, C=RWKV_CHUNK)
    Lp = SAMPLE_ROWS
    pad = lambda t: jnp.pad(t[Mp:Mp + Bs].reshape(Bs, 1, D), ((0, 0), (0, Lp - 1), (0, 0))).reshape(Bs * Lp, D)
    y_s, s_sample = rwkv_recurrence(*[pad(t) for t in (r, k, v, lw, a, g)], *params, s0_sample, B=Bs, L=Lp, C=Lp)
    y = _with_sample_rows(y, y_s.reshape(Bs, Lp, D)[:, 0], Mp)
    pre1 = mm(y, [(w["o"], 0)], n=D, tn=TN_STAT, extras=(x32,), epi=lambda acc, xt: ALPHA * xt + acc,
              name="rwkv_o")
    return pre1, s_prompt, s_sample


def kernel(x_prompt, x_sample, p_prompt, p_sample, cache_moba_k, cache_moba_v, page_table, state_ret, state_wkv, state_shift, ln1_g, ln1_b, ln2_g, ln2_b, w_ffn_in, w_ffn_out, w_ple_gate, w_ple_proj, rel_bias, a_w_qkv, a_w_o, r_w_in, r_gn_g, r_w_o, c_mu, c_w_rkv, c_w0, c_w1, c_w2, c_a0, c_a1, c_a2, c_g1, c_g2, c_k_k, c_k_a, c_r_k, c_gn_g, c_gn_b, c_w_o):
    Bp, Sp, D = x_prompt.shape
    Bs, Ts, _ = x_sample.shape
    assert Ts == 1 and Bs <= SAMPLE_REGION
    Mp = Bp * Sp
    hd = D // A_HEADS
    past = page_table.shape[1] * PAGE_SIZE
    bf = lambda t: t.astype(BF16)
    rows = lambda prompt, sample: jnp.concatenate(
        [prompt, sample, jnp.zeros((SAMPLE_REGION - sample.shape[0], prompt.shape[1]), prompt.dtype)], axis=0)
    dims = dict(Bp=Bp, Sp=Sp, Bs=Bs)

    x32 = rows(x_prompt.reshape(Mp, D), x_sample.reshape(Bs, D))
    x16 = bf(x32)
    gpad = W_GATE_LORA_PAD - c_g1.shape[-1]
    g1 = jnp.pad(c_g1, ((0, 0), (0, 0), (0, gpad)))
    g2 = jnp.pad(c_g2, ((0, 0), (0, gpad), (0, 0)))
    w_ffn_out16, r_w_o16 = bf(w_ffn_out), bf(r_w_o)

    kp_l, vp_l, ks_l, vs_l, rp_l, rs_l, wp_l, ws_l, hp_l, hs_l = ([] for _ in range(10))
    for i in range(DEPTH):
        kind, j = i % 3, i // 3
        tail = dict(ln1_g=ln1_g[i], ln1_b=ln1_b[i], ln2_g=ln2_g[i], ln2_b=ln2_b[i],
                    ffn_in=(w_ffn_in, (i,)), ffn_out=(w_ffn_out16, (i,)),
                    ple_gate=(w_ple_gate, (i,)), ple_proj=(w_ple_proj, (i,)))
        if kind == 0:
            w = dict(qkv=(a_w_qkv, (j,)), o=(a_w_o, (j,)))
            pre1, k, v = _moba_layer(x32, x16, w, rel_bias, cache_moba_k, cache_moba_v, j, page_table, **dims)
            kp_l.append(k[:Mp].reshape(Bp, Sp, A_HEADS, hd))
            vp_l.append(v[:Mp].reshape(Bp, Sp, A_HEADS, hd))
            ks_l.append(k[Mp:Mp + Bs].reshape(Bs, Ts, A_HEADS, hd))
            vs_l.append(v[Mp:Mp + Bs].reshape(Bs, Ts, A_HEADS, hd))
        elif kind == 1:
            w = dict(o=(r_w_o16, (j,)), gn_g=r_gn_g[j])
            w["in"] = (r_w_in, (j,))
            pre1, s_prompt, s_sample = _ret_layer(x32, x16, w, state_ret[j], past, **dims)
            rp_l.append(s_prompt)
            rs_l.append(s_sample)
        else:
            w = dict(mu=c_mu[j], w0=c_w0[j].reshape(1, D), a0=c_a0[j].reshape(1, D), k_k=c_k_k[j], k_a=c_k_a[j],
                     r_k=c_r_k[j], gn_g=c_gn_g[j], gn_b=c_gn_b[j], rkv=(c_w_rkv, (j,)), w1=(c_w1, (j,)),
                     w2=(c_w2, (j,)), a1=(c_a1, (j,)), a2=(c_a2, (j,)), g1=(g1, (j,)), g2=(g2, (j,)),
                     o=(c_w_o, (j,)))
            xp3 = x32[:Mp].reshape(Bp, Sp, D)
            x_prev = rows(jnp.concatenate([jnp.zeros((Bp, 1, D), F32), xp3[:, :-1]], axis=1).reshape(Mp, D),
                          state_shift[j])
            pre1, s_prompt, s_sample = _rwkv_layer(x32, x_prev, w, state_wkv[j], **dims)
            wp_l.append(s_prompt)
            ws_l.append(s_sample)
            hp_l.append(xp3[:, -1])
            hs_l.append(x32[Mp:Mp + Bs])
        p16 = bf(rows(p_prompt[i].reshape(Mp, -1), p_sample[i].reshape(Bs, -1)))
        x32, x16 = _layer_tail(pre1, p16, tail)
    return (x32[:Mp].reshape(Bp, Sp, D), x32[Mp:Mp + Bs].reshape(Bs, Ts, D),
            jnp.stack(kp_l), jnp.stack(vp_l), jnp.stack(ks_l), jnp.stack(vs_l),
            jnp.stack(rp_l), jnp.stack(rs_l), jnp.stack(wp_l), jnp.stack(ws_l),
            jnp.stack(hp_l), jnp.stack(hs_l))
```

```python
import functools
import math

import numpy as np
import jax
import jax.numpy as jnp
from jax import lax
from jax.experimental import pallas as pl
from jax.experimental.pallas import tpu as pltpu

F32 = jnp.float32
BF16 = jnp.bfloat16

DEPTH = 4
LN_EPS = 1e-5
ALPHA = (2 * DEPTH) ** 0.25
A_HEADS = 32
MOBA_BLOCK = 256
MOBA_TOPK = 3
PAGE_SIZE = 128
REL_BUCKETS = 32
REL_MAX_DIST = 128
RET_HEADS = 16
RET_CHUNK = 128
RET_GN_EPS = 1e-5
W_HEAD = 64
W_GN_EPS = 64e-5
W_GATE_LORA_PAD = 512
NEG_INF = -1e30
LOG2E = math.log2(math.e)

V7X_VMEM_LIMIT_BYTES = 56 * 1024 * 1024
LANES = 128

MOBA_HEADS_PER_STEP = 4
RET_HEADS_PER_STEP = 4
RWKV_CHUNK = 64
RWKV_HEADS_PER_STEP = 16
SAMPLE_ROWS = 16


def _t5_bucket_thresholds():
    n = np.arange(0, 4 * REL_MAX_DIST)
    max_exact = REL_BUCKETS // 2
    nf = np.maximum(n, 1).astype(np.float32)
    large = max_exact + (np.log(nf / np.float32(max_exact)) / np.float32(math.log(REL_MAX_DIST / max_exact))
                         * np.float32(REL_BUCKETS - max_exact)).astype(np.int32)
    bucket = np.where(n < max_exact, n, np.minimum(large, REL_BUCKETS - 1))
    assert sorted(set(bucket.tolist())) == list(range(REL_BUCKETS))
    return tuple(int(np.argmax(bucket >= b)) for b in range(REL_BUCKETS))


T5_THRESHOLDS = _t5_bucket_thresholds()


def _params(*sem):
    return pltpu.CompilerParams(dimension_semantics=sem, vmem_limit_bytes=V7X_VMEM_LIMIT_BYTES)


def _bias_from_distance(dist, table_ref, h):
    t = jnp.full(dist.shape, table_ref[0, h], F32)
    for b in range(1, REL_BUCKETS):
        t = jnp.where(dist >= T5_THRESHOLDS[b], table_ref[b, h], t)
    return t


def _mm_kernel(epi, n_w, n_ex, x_ref, *refs):
    w_refs = refs[:n_w]
    ex_refs = refs[n_w:n_w + n_ex]
    o_refs = refs[n_w + n_ex:]
    x = x_ref[...]
    accs = [jnp.dot(x, w[...], preferred_element_type=F32) for w in w_refs]
    outs = epi(*accs, *[e[...] for e in ex_refs])
    if not isinstance(outs, tuple):
        outs = (outs,)
    for o_ref, o in zip(o_refs, outs):
        o_ref[...] = o.astype(o_ref.dtype)


def matmul(x, ws, *, n, tm, tn, epi=None, extras=(), out_dtypes=(F32,), x_buffers=2, name="mm"):
    M, K = x.shape
    assert M % tm == 0 and n % tn == 0, (M, tm, n, tn)
    if epi is None:
        epi = lambda a: a
    in_specs = [pl.BlockSpec((tm, K), lambda i, j: (i, 0), pipeline_mode=pl.Buffered(x_buffers))]
    args = [x]
    for w, col0 in ws:
        w, lead = w if isinstance(w, tuple) else (w, ())
        assert w.shape[-2] == K and col0 % tn == 0 and w.ndim == len(lead) + 2
        in_specs.append(pl.BlockSpec((None,) * len(lead) + (K, tn),
                                     lambda i, j, cb=col0 // tn, lead=lead: lead + (0, j + cb)))
        args.append(w)
    for e in extras:
        if isinstance(e, tuple):
            row, col0 = e
            assert row.shape[0] == 1 and col0 % tn == 0
            in_specs.append(pl.BlockSpec((1, tn), lambda i, j, cb=col0 // tn: (0, j + cb)))
            args.append(row)
        else:
            assert e.shape == (M, n)
            in_specs.append(pl.BlockSpec((tm, tn), lambda i, j: (i, j)))
            args.append(e)
    out_shape = tuple(jax.ShapeDtypeStruct((M, n), dt) for dt in out_dtypes)
    out_specs = tuple(pl.BlockSpec((tm, tn), lambda i, j: (i, j)) for _ in out_dtypes)
    res = pl.pallas_call(
        functools.partial(_mm_kernel, epi, len(ws), len(extras)),
        grid=(M // tm, n // tn),
        in_specs=in_specs,
        out_specs=out_specs,
        out_shape=out_shape,
        compiler_params=_params("arbitrary", "arbitrary"),
        name=name,
    )(*args)
    return res if len(out_dtypes) > 1 else res[0]


def _ln_kernel(x_ref, g_ref, b_ref, o32_ref, o16_ref):
    x = x_ref[...]
    mu = jnp.mean(x, axis=-1, keepdims=True)
    xc = x - mu
    var = jnp.mean(xc * xc, axis=-1, keepdims=True)
    y = xc * lax.rsqrt(var + LN_EPS) * g_ref[...] + b_ref[...]
    o32_ref[...] = y
    o16_ref[...] = y.astype(BF16)


def layer_norm(x, g, b, *, tm):
    M, D = x.shape
    row = pl.BlockSpec((1, D), lambda i: (0, 0))
    tile = pl.BlockSpec((tm, D), lambda i: (i, 0))
    return pl.pallas_call(
        _ln_kernel,
        grid=(M // tm,),
        in_specs=[tile, row, row],
        out_specs=(tile, tile),
        out_shape=(jax.ShapeDtypeStruct((M, D), F32), jax.ShapeDtypeStruct((M, D), BF16)),
        compiler_params=_params("arbitrary"),
        name="layer_norm",
    )(x, g.reshape(1, D), b.reshape(1, D))


def _moba_prompt_kernel(nb, G, tab_ref, q_ref, k_ref, v_ref, o_ref, bown, bprev, kmean, vt, m_s, l_s, acc_s):
    hg = pl.program_id(0)
    b = pl.program_id(1)
    qb = pl.program_id(2)
    blk = MOBA_BLOCK
    n_cand = nb - 1
    k_sel_n = min(MOBA_TOPK, n_cand)
    hd = q_ref.shape[1] // G
    score_scale = hd ** -0.5 * LOG2E
    h0 = hg * G
    key = lax.broadcasted_iota(jnp.int32, (blk, blk), 0)
    qry = lax.broadcasted_iota(jnp.int32, (blk, blk), 1)
    heads = lambda x: jnp.stack([x[:, g * hd:(g + 1) * hd] for g in range(G)])

    @pl.when((b == 0) & (qb == 0))
    def _():
        for g in range(G):
            bown[g] = _bias_from_distance(jnp.maximum(qry - key, 0), tab_ref, h0 + g) * LOG2E
            bprev[g] = _bias_from_distance(qry - key + blk, tab_ref, h0 + g) * LOG2E

    @pl.when(qb == 0)
    def _():
        kmean[...] = jnp.zeros(kmean.shape, F32)
        for n in range(n_cand):
            mean_n = jnp.mean(k_ref[n * blk:(n + 1) * blk, :], axis=0, keepdims=True)
            for g in range(G):
                kmean[g, n:n + 1, :] = mean_n[:, g * hd:(g + 1) * hd]
        for n in range(nb):
            for g in range(G):
                vt[n, g] = v_ref[n * blk:(n + 1) * blk, g * hd:(g + 1) * hd].T.astype(BF16)

    q = heads(q_ref[...])
    q16 = q.astype(BF16)
    dn_kq = (((2,), (2,)), ((0,), (0,)))
    dn_vp = (((2,), (1,)), ((0,), (0,)))

    start = pl.multiple_of(qb * blk, blk)
    k_own = heads(k_ref[pl.ds(start, blk), :]).astype(BF16)
    s = lax.dot_general(k_own, q16, dn_kq, preferred_element_type=F32) * score_scale + bown[...]
    s = jnp.where(qry >= key, s, NEG_INF)
    m0 = jnp.max(s, axis=1, keepdims=True)
    p = jnp.exp2(s - m0)
    m_s[...] = m0
    l_s[...] = jnp.sum(p, axis=1, keepdims=True)
    acc_s[...] = lax.dot_general(vt[qb], p.astype(BF16), dn_vp, preferred_element_type=F32)

    if n_cand > 0:
        ncols = kmean.shape[1]
        gate = lax.dot_general(kmean[...], q, dn_kq, precision=lax.Precision.HIGHEST,
                               preferred_element_type=F32)
        cand = lax.broadcasted_iota(jnp.int32, (G, ncols, blk), 1)
        gate = jnp.where(cand < qb, gate, NEG_INF)
        rank = jnp.zeros((G, ncols, blk), F32)
        for m in range(n_cand):
            gm = gate[:, m:m + 1, :]
            ahead = (gm > gate) | ((gm == gate) & (cand > m))
            rank = rank + jnp.where(ahead, 1.0, 0.0)
        sel = jnp.where((rank < k_sel_n) & (cand < qb), 1.0, 0.0)
        gid = lax.broadcasted_iota(jnp.int32, (G, 1, 1), 0)
        far_bias = jnp.full((G, 1, 1), tab_ref[REL_BUCKETS - 1, h0], F32)
        for g in range(1, G):
            far_bias = jnp.where(gid == g, tab_ref[REL_BUCKETS - 1, h0 + g], far_bias)
        far_bias = far_bias * LOG2E

        def attend(n, just_before):
            kn = heads(k_ref[n * blk:(n + 1) * blk, :]).astype(BF16)
            bias = bprev[...] if just_before else far_bias
            sn = lax.dot_general(kn, q16, dn_kq, preferred_element_type=F32) * score_scale + bias
            sn = jnp.where(sel[:, n:n + 1, :] > 0.5, sn, NEG_INF)
            m_old = m_s[...]
            m_new = jnp.maximum(m_old, jnp.max(sn, axis=1, keepdims=True))
            scale_old = jnp.exp2(m_old - m_new)
            pn = jnp.exp2(sn - m_new)
            l_s[...] = scale_old * l_s[...] + jnp.sum(pn, axis=1, keepdims=True)
            acc_s[...] = scale_old * acc_s[...] + lax.dot_general(vt[n], pn.astype(BF16), dn_vp,
                                                                  preferred_element_type=F32)
            m_s[...] = m_new

        for n in range(n_cand):
            pl.when(n == qb - 1)(functools.partial(attend, n, True))
            pl.when(n < qb - 1)(functools.partial(attend, n, False))

    out = acc_s[...] / l_s[...]
    for g in range(G):
        o_ref[:, g * hd:(g + 1) * hd] = out[g].T.astype(o_ref.dtype)


def moba_prompt(q, k, v, rel_bias, *, B, S, H):
    M, D = q.shape
    hd = D // H
    blk = MOBA_BLOCK
    G = math.gcd(H, MOBA_HEADS_PER_STEP)
    assert S % blk == 0 and M == B * S and T5_THRESHOLDS[-1] <= blk
    nb = S // blk
    ncols = max(8, -(-(nb - 1) // 8) * 8)
    qspec = pl.BlockSpec((blk, G * hd), lambda hg, b, qb: (b * nb + qb, hg))
    kvspec = pl.BlockSpec((S, G * hd), lambda hg, b, qb: (b, hg))
    return pl.pallas_call(
        functools.partial(_moba_prompt_kernel, nb, G),
        grid=(H // G, B, nb),
        in_specs=[pl.BlockSpec(memory_space=pltpu.SMEM), qspec, kvspec, kvspec],
        out_specs=qspec,
        out_shape=jax.ShapeDtypeStruct((M, D), BF16),
        scratch_shapes=[pltpu.VMEM((G, blk, blk), F32), pltpu.VMEM((G, blk, blk), F32),
                        pltpu.VMEM((G, ncols, hd), F32), pltpu.VMEM((nb, G, hd, blk), BF16),
                        pltpu.VMEM((G, 1, blk), F32), pltpu.VMEM((G, 1, blk), F32),
                        pltpu.VMEM((G, hd, blk), F32)],
        compiler_params=_params("arbitrary", "arbitrary", "arbitrary"),
        name="moba_prompt",
    )(rel_bias, q, k, v)


def _moba_select_kernel(n_blocks, pt_ref, pa_ref, pb_ref, q_ref, sel_ref, sums):
    j = pl.program_id(1)
    sums[j] = jnp.sum(pa_ref[0, 0], axis=0) + jnp.sum(pb_ref[0, 0], axis=0)

    @pl.when(j == n_blocks - 1)
    def _():
        prod = (sums[...] * (1.0 / MOBA_BLOCK)) * q_ref[...]
        gate = jnp.sum(prod, axis=-1, keepdims=True)
        blk_id = lax.broadcasted_iota(jnp.int32, gate.shape, 0)
        rank = jnp.zeros(gate.shape, F32)
        for m in range(n_blocks):
            gm = gate[m:m + 1]
            ahead = (gm > gate) | ((gm == gate) & (blk_id > m))
            rank = rank + jnp.where(ahead, 1.0, 0.0)
        for r in range(sel_ref.shape[1]):
            sel_ref[0, r] = jnp.sum(jnp.where(rank == float(r), blk_id, 0), axis=0)


def _moba_sample_attn_kernel(n_sel_pages, past, layer, n_pages, pt_ref, sel_ref, tab_ref, q_ref, kn_ref, vn_ref,
                             k_hbm, v_hbm, o_ref, kbuf, vbuf, sem):
    b = pl.program_id(0)
    h = pl.program_id(1)
    n_heads = pl.num_programs(1)
    step = b * n_heads + h
    n_steps = pl.num_programs(0) * n_heads
    hd = q_ref.shape[-1]
    a_scale = hd ** -0.5
    pages_per_block = MOBA_BLOCK // PAGE_SIZE

    def page_copies(s, slot):
        bb = lax.div(s, n_heads)
        hh = lax.rem(s, n_heads)
        out = []
        for t in range(n_sel_pages):
            blk = sel_ref[s * MOBA_TOPK + t // pages_per_block]
            page = pt_ref[bb * n_pages + pages_per_block * blk + t % pages_per_block]
            out.append(pltpu.make_async_copy(k_hbm.at[layer, page, :, hh, :], kbuf.at[slot, t], sem.at[0, slot, t]))
            out.append(pltpu.make_async_copy(v_hbm.at[layer, page, :, hh, :], vbuf.at[slot, t], sem.at[1, slot, t]))
        return out

    slot = lax.rem(step, 2)

    @pl.when(step == 0)
    def _():
        for c in page_copies(step, slot):
            c.start()

    @pl.when(step + 1 < n_steps)
    def _():
        for c in page_copies(step + 1, 1 - slot):
            c.start()

    for c in page_copies(step, slot):
        c.wait()

    q = q_ref[0]
    s_own = jnp.sum(q * kn_ref[0], axis=-1, keepdims=True) * a_scale + tab_ref[0, h]
    r = lax.broadcasted_iota(jnp.int32, (PAGE_SIZE, 1), 0)
    scores = []
    for t in range(n_sel_pages):
        blk = sel_ref[step * MOBA_TOPK + t // pages_per_block]
        pos0 = blk * MOBA_BLOCK + (t % pages_per_block) * PAGE_SIZE
        dist = past - (pos0 + r)
        bias = _bias_from_distance(dist, tab_ref, h)
        scores.append(jnp.sum(kbuf[slot, t] * q, axis=-1, keepdims=True) * a_scale + bias)
    m = s_own
    for s in scores:
        m = jnp.maximum(m, jnp.max(s, axis=0, keepdims=True))
    p_own = jnp.exp(s_own - m)
    l = p_own
    acc = p_own * vn_ref[0]
    for t, s in enumerate(scores):
        p = jnp.exp(s - m)
        l = l + jnp.sum(p, axis=0, keepdims=True)
        acc = acc + jnp.sum(p * vbuf[slot, t], axis=0, keepdims=True)
    o_ref[0] = (acc / l).astype(o_ref.dtype)


def moba_sample(q, k_new, v_new, k_pool, v_pool, layer, page_table, rel_bias, *, H):
    rows, D = q.shape
    B, n_pages = page_table.shape
    assert rows == B
    hd = D // H
    past = n_pages * PAGE_SIZE
    ppb = MOBA_BLOCK // PAGE_SIZE
    assert past % MOBA_BLOCK == 0
    n_blocks = past // MOBA_BLOCK
    assert n_blocks >= MOBA_TOPK
    pt = page_table.reshape(-1)
    q3 = q.reshape(rows, 1, D)

    page = lambda half: pl.BlockSpec(
        (1, 1, PAGE_SIZE, H, hd), lambda b, j, pt_ref: (layer, pt_ref[b * n_pages + ppb * j + half], 0, 0, 0))
    sel = pl.pallas_call(
        functools.partial(_moba_select_kernel, n_blocks),
        grid_spec=pltpu.PrefetchScalarGridSpec(
            num_scalar_prefetch=1,
            grid=(B, n_blocks),
            in_specs=[page(0), page(1), pl.BlockSpec((1, H, hd), lambda b, j, pt_ref: (b, 0, 0))],
            out_specs=pl.BlockSpec((1, MOBA_TOPK, H, 1), lambda b, j, pt_ref: (b, 0, 0, 0)),
            scratch_shapes=[pltpu.VMEM((n_blocks, H, hd), F32)]),
        out_shape=jax.ShapeDtypeStruct((B, MOBA_TOPK, H, 1), jnp.int32),
        compiler_params=_params("arbitrary", "arbitrary"),
        name="moba_sample_select",
    )(pt, k_pool, k_pool, q.reshape(B, H, hd))
    sel_flat = jnp.transpose(sel[..., 0], (0, 2, 1)).reshape(-1)

    n_sel_pages = MOBA_TOPK * ppb
    tok = pl.BlockSpec((1, 1, hd), lambda b, h, pt_ref, sel_ref: (b, 0, h))
    hbm = pl.BlockSpec(memory_space=pl.ANY)
    out = pl.pallas_call(
        functools.partial(_moba_sample_attn_kernel, n_sel_pages, past, layer, n_pages),
        grid_spec=pltpu.PrefetchScalarGridSpec(
            num_scalar_prefetch=2,
            grid=(B, H),
            in_specs=[pl.BlockSpec(memory_space=pltpu.SMEM), tok, tok, tok, hbm, hbm],
            out_specs=tok,
            scratch_shapes=[pltpu.VMEM((2, n_sel_pages, PAGE_SIZE, hd), F32),
                            pltpu.VMEM((2, n_sel_pages, PAGE_SIZE, hd), F32),
                            pltpu.SemaphoreType.DMA((2, 2, n_sel_pages))]),
        out_shape=jax.ShapeDtypeStruct((rows, 1, D), BF16),
        compiler_params=_params("arbitrary", "arbitrary"),
        name="moba_sample_attn",
    )(pt, sel_flat, rel_bias, q3, k_new.reshape(rows, 1, D), v_new.reshape(rows, 1, D), k_pool, v_pool)
    return out.reshape(rows, D)


def _retention_kernel(c_real, lg_ref, q_ref, k_ref, v_ref, g_ref, cos_ref, sin_ref, gn_ref, s0_ref,
                      y_ref, s_out_ref, s_acc):
    hg = pl.program_id(1)
    c = pl.program_id(2)
    n_chunks = pl.num_programs(2)
    C, dk = cos_ref.shape
    G = q_ref.shape[1] // dk
    dv = v_ref.shape[1] // G

    @pl.when(c == 0)
    def _():
        s_acc[...] = s0_ref[0]

    gid = lax.broadcasted_iota(jnp.int32, (G, 1, 1), 0)
    lg = jnp.full((G, 1, 1), lg_ref[hg * G], F32)
    for g in range(1, G):
        lg = jnp.where(gid == g, lg_ref[hg * G + g], lg)
    cos = cos_ref[...]
    sin = sin_ref[...]
    even = (lax.broadcasted_iota(jnp.int32, (C, LANES), 1) % 2) == 0
    heads = lambda x, w: jnp.stack([x[:, g * w:(g + 1) * w] for g in range(G)])

    def rotate(x):
        tiles = []
        for t in range(dk // LANES):
            xt = x[:, t * LANES:(t + 1) * LANES]
            tiles.append(jnp.where(even, pltpu.roll(xt, LANES - 1, 1), pltpu.roll(xt, 1, 1)))
        return x * cos + jnp.concatenate(tiles, axis=1) * sin

    q = jnp.stack([rotate(q_ref[:, g * dk:(g + 1) * dk]) for g in range(G)])
    k = jnp.stack([rotate(k_ref[:, g * dk:(g + 1) * dk]) for g in range(G)]) * (dk ** -0.5)
    v16 = heads(v_ref[...], dv).astype(BF16)
    idx = lax.broadcasted_iota(jnp.int32, (1, C, 1), 1).astype(F32)
    q_dec = jnp.exp((idx + 1.0) * lg)
    k_dec = jnp.exp((c_real - 1.0 - idx) * lg)
    ii = lax.broadcasted_iota(jnp.int32, (1, C, C), 1)
    jj = lax.broadcasted_iota(jnp.int32, (1, C, C), 2)
    diff = (ii - jj).astype(F32)
    dmask = jnp.where(diff >= 0, jnp.exp(jnp.maximum(diff, 0.0) * lg), 0.0)
    dg = lambda a, b, dn: lax.dot_general(a.astype(BF16), b.astype(BF16), dn, preferred_element_type=F32)
    qk = dg(q, k, _DN_NT) * dmask
    s_prev = s_acc[...]
    o = dg(qk, v16, _DN_NN) + dg(q * q_dec, s_prev, _DN_NN)
    kv = dg(k * k_dec, v16, _DN_TN)
    c_dec = jnp.exp(c_real * lg)
    s_new = s_prev * c_dec + kv
    s_acc[...] = s_new

    mu = jnp.mean(o, axis=-1, keepdims=True)
    oc = o - mu
    var = jnp.mean(oc * oc, axis=-1, keepdims=True)
    gate = heads(g_ref[...], dv)
    y = oc * lax.rsqrt(var + RET_GN_EPS) * heads(gn_ref[...], dv) * (gate * jax.nn.sigmoid(gate))
    for g in range(G):
        y_ref[:, g * dv:(g + 1) * dv] = y[g].astype(y_ref.dtype)

    @pl.when(c == n_chunks - 1)
    def _():
        s_out_ref[0] = s_new


def retention(qkvg, s0, gn_g, cos_t, sin_t, *, B, L, H, C, c_real):
    M, N6 = qkvg.shape
    D = N6 // 6
    dk, dv = D // H, 2 * D // H
    assert L % C == 0 and M == B * L
    nC = L // C
    G = math.gcd(H, RET_HEADS_PER_STEP)
    nG = H // G
    log_g = jnp.log1p(-jnp.exp2(-5.0 - jnp.arange(H, dtype=F32)))
    tokspec = lambda width, sec: pl.BlockSpec((C, G * width), lambda b, hg, c: (b * nC + c, sec * nG + hg))
    tabspec = pl.BlockSpec((C, dk), lambda b, hg, c: (c, 0))
    sspec = pl.BlockSpec((1, G, dk, dv), lambda b, hg, c: (b, hg, 0, 0))
    return pl.pallas_call(
        functools.partial(_retention_kernel, float(c_real)),
        grid=(B, nG, nC),
        in_specs=[pl.BlockSpec(memory_space=pltpu.SMEM),
                  tokspec(dk, 0), tokspec(dk, 1), tokspec(dv, 1), tokspec(dv, 2),
                  tabspec, tabspec, pl.BlockSpec((1, G * dv), lambda b, hg, c: (0, hg)), sspec],
        out_specs=(pl.BlockSpec((C, G * dv), lambda b, hg, c: (b * nC + c, hg)), sspec),
        out_shape=(jax.ShapeDtypeStruct((M, 2 * D), BF16), jax.ShapeDtypeStruct(s0.shape, F32)),
        scratch_shapes=[pltpu.VMEM((G, dk, dv), F32)],
        compiler_params=_params("arbitrary", "arbitrary", "arbitrary"),
        name="retention",
    )(log_g, qkvg, qkvg, qkvg, qkvg, cos_t, sin_t, gn_g.reshape(1, 2 * D), s0)


def _rotation_tables(pos, dk):
    inv = 1.0 / (10000.0 ** jnp.linspace(0.0, 1.0, dk // 2, dtype=F32))
    ang = pos.astype(F32)[:, None] * inv
    cos = jnp.repeat(jnp.cos(ang), 2, axis=1)
    sin = jnp.stack([-jnp.sin(ang), jnp.sin(ang)], axis=-1).reshape(pos.shape[0], dk)
    return cos, sin


def _shift_mix_kernel(x_ref, xp_ref, mu_ref, *o_refs):
    x = x_ref[...]
    xx = xp_ref[...] - x
    for i, o_ref in enumerate(o_refs):
        o_ref[...] = (x + xx * mu_ref[i:i + 1, :]).astype(o_ref.dtype)


def shift_mix(x, x_prev, mu, *, tm):
    M, D = x.shape
    n = mu.shape[0]
    tile = pl.BlockSpec((tm, D), lambda i: (i, 0))
    return pl.pallas_call(
        _shift_mix_kernel,
        grid=(M // tm,),
        in_specs=[tile, tile, pl.BlockSpec((n, D), lambda i: (0, 0))],
        out_specs=tuple(tile for _ in range(n)),
        out_shape=tuple(jax.ShapeDtypeStruct((M, D), BF16) for _ in range(n)),
        compiler_params=_params("arbitrary"),
        name="rwkv_shift_mix",
    )(x, x_prev, mu)


def _split3(x):
    hi = x.astype(BF16)
    r1 = x - hi.astype(F32)
    mid = r1.astype(BF16)
    lo = (r1 - mid.astype(F32)).astype(BF16)
    return hi, mid, lo


def _dot16(a, b, dn):
    return lax.dot_general(a.astype(BF16), b.astype(BF16), dimension_numbers=dn, preferred_element_type=F32)


_DN_NN = (((2,), (1,)), ((0,), (0,)))
_DN_NT = (((2,), (2,)), ((0,), (0,)))
_DN_TN = (((1,), (1,)), ((0,), (0,)))


def _rwkv_kernel(r_ref, k_ref, v_ref, lw_ref, a_ref, g_ref, kk_ref, ka_ref, rk_ref, gng_ref, gnb_ref, s0_ref,
                 y_ref, s_out_ref, s_acc):
    c = pl.program_id(2)
    n_chunks = pl.num_programs(2)
    C = r_ref.shape[0]
    N = W_HEAD
    G = r_ref.shape[1] // N

    @pl.when(c == 0)
    def _():
        s_acc[...] = s0_ref[0]

    ti = lax.broadcasted_iota(jnp.int32, (C, C), 0)
    tj = lax.broadcasted_iota(jnp.int32, (C, C), 1)
    strict = ti > tj
    incl = ti >= tj
    tri16 = jnp.broadcast_to(jnp.where(incl, 1.0, 0.0).astype(BF16), (G, C, C))
    n_double = max(1, (C - 1).bit_length())

    heads = lambda ref: jnp.stack([ref[:, hh * N:(hh + 1) * N] for hh in range(G)])
    r, k, v, lw, a = heads(r_ref), heads(k_ref), heads(v_ref), heads(lw_ref), heads(a_ref)
    kk = k * heads(kk_ref)
    kk = kk / jnp.maximum(jnp.sqrt(jnp.sum(kk * kk, axis=-1, keepdims=True)), 1e-12)
    kh = k * (1.0 + (a - 1.0) * heads(ka_ref))
    av = -kk
    bv = kk * a
    dg = functools.partial(lax.dot_general, dimension_numbers=_DN_NN, preferred_element_type=F32)
    hi, mid, lo = _split3(lw)
    cum = dg(tri16, hi) + dg(tri16, mid) + dg(tri16, lo)
    e_in = jnp.exp(cum)
    e_ex = jnp.exp(cum - lw)
    e_neg = jnp.exp(-cum)
    at = av * e_ex
    rt = r * e_in
    bt = bv * e_neg
    kt = kh * e_neg
    s0 = s_acc[...]
    lh = jnp.concatenate([at, rt], axis=1)
    m_b = _dot16(lh, bt, _DN_NT)
    m_k = _dot16(lh, kt, _DN_NT)
    z = _dot16(lh, s0, _DN_NT)
    a_ab = jnp.where(strict, m_b[:, :C], 0.0)
    a_ak = jnp.where(strict, m_k[:, :C], 0.0)
    a_rb = jnp.where(incl, m_b[:, C:], 0.0)
    a_rk = jnp.where(incl, m_k[:, C:], 0.0)
    u = z[:, :C] + _dot16(a_ak, v, _DN_NN)
    pw = a_ab
    for it in range(n_double):
        u = u + _dot16(pw, u, _DN_NN)
        if it + 1 < n_double:
            pw = _dot16(pw, pw, _DN_NN)
    y = z[:, C:] + _dot16(a_rb, u, _DN_NN) + _dot16(a_rk, v, _DN_NN)
    wc = e_in[:, C - 1:C, :]
    s_acc[...] = s0 * wc + _dot16(u, bt * wc, _DN_TN) + _dot16(v, kt * wc, _DN_TN)

    mu = jnp.mean(y, axis=-1, keepdims=True)
    yc = y - mu
    var = jnp.mean(yc * yc, axis=-1, keepdims=True)
    yn = yc * lax.rsqrt(var + W_GN_EPS)
    bonus = jnp.sum(r * kh * heads(rk_ref), axis=-1, keepdims=True) * v
    out = (yn * heads(gng_ref) + heads(gnb_ref) + bonus) * heads(g_ref)
    for hh in range(G):
        y_ref[:, hh * N:(hh + 1) * N] = out[hh].astype(y_ref.dtype)

    @pl.when(c == n_chunks - 1)
    def _():
        s_out_ref[0] = s_acc[...]


def rwkv_recurrence(r, k, v, lw, a, g, k_k, k_a, r_k, gn_g, gn_b, s0, *, B, L, C):
    M, D = r.shape
    N = W_HEAD
    H = D // N
    G = math.gcd(H, RWKV_HEADS_PER_STEP)
    assert L % C == 0 and M == B * L
    nC = L // C
    tok = pl.BlockSpec((C, G * N), lambda b, hg, c: (b * nC + c, hg))
    row = pl.BlockSpec((1, G * N), lambda b, hg, c: (0, hg))
    sspec = pl.BlockSpec((1, G, N, N), lambda b, hg, c: (b, hg, 0, 0))
    rows = [p.reshape(1, D) for p in (k_k, k_a, r_k, gn_g, gn_b)]
    return pl.pallas_call(
        _rwkv_kernel,
        grid=(B, H // G, nC),
        in_specs=[tok] * 6 + [row] * 5 + [sspec],
        out_specs=(tok, sspec),
        out_shape=(jax.ShapeDtypeStruct((M, D), BF16), jax.ShapeDtypeStruct(s0.shape, F32)),
        scratch_shapes=[pltpu.VMEM((G, N, N), F32)],
        compiler_params=_params("arbitrary", "arbitrary", "arbitrary"),
        name="rwkv_recurrence",
    )(r, k, v, lw, a, g, *rows, s0)


def _tiles(M):
    if M >= 1024:
        return 1024, 256
    return M, M


TN_WIDE = 1024
TN_FFN = 256


def _mat_shape(w):
    return (w[0] if isinstance(w, tuple) else w).shape[-2:]


def _layer_tail(pre1, p16, w, *, M):
    tm, tr = _tiles(M)
    F, D = _mat_shape(w["ffn_out"])
    h32, h16 = layer_norm(pre1, w["ln1_g"], w["ln1_b"], tm=tr)
    act = matmul(h16, [(w["ffn_in"], 0), (w["ffn_in"], F)], n=F, tm=min(M, 2 * tm), tn=TN_FFN,
                 epi=lambda gt, up: gt * jax.nn.sigmoid(gt) * up, out_dtypes=(BF16,), name="ffn_in")
    pp = matmul(p16, [(w["ple_proj"], 0)], n=D, tm=tm, tn=TN_WIDE, name="ple_proj")
    ple = matmul(h16, [(w["ple_gate"], 0)], n=D, tm=tm, tn=512, extras=(pp,),
                 epi=lambda acc, ppt: jax.nn.sigmoid(acc) * ppt, name="ple_gate")
    pre2 = matmul(act, [(w["ffn_out"], 0)], n=D, tm=tm, tn=TN_FFN, extras=(ple, h32), x_buffers=1,
                  epi=lambda acc, plet, ht: ALPHA * ht + acc + plet, name="ffn_out")
    return layer_norm(pre2, w["ln2_g"], w["ln2_b"], tm=tr)


def _moba_layer(x32, x16, w, rel_bias, *, M, prompt, B, S=None, k_pool=None, v_pool=None, layer=None,
                page_table=None):
    tm, _ = _tiles(M)
    D = x32.shape[1]
    q, k, v = [matmul(x16, [(w["qkv"], i * D)], n=D, tm=tm, tn=TN_WIDE, name="moba_qkv") for i in range(3)]
    if prompt:
        att = moba_prompt(q, k, v, rel_bias, B=B, S=S, H=A_HEADS)
    else:
        att = moba_sample(q[:B], k[:B], v[:B], k_pool, v_pool, layer, page_table, rel_bias, H=A_HEADS)
        att = jnp.pad(att, ((0, M - B), (0, 0)))
    pre1 = matmul(att, [(w["o"], 0)], n=D, tm=tm, tn=512, extras=(x32,),
                  epi=lambda acc, xt: ALPHA * xt + acc, name="moba_o")
    return pre1, k, v


def _ret_layer(x32, x16, w, s0, pos0, *, M, B, L):
    tm, _ = _tiles(M)
    D = x32.shape[1]
    dk = D // RET_HEADS
    qkvg = matmul(x16, [(w["in"], 0)], n=6 * D, tm=tm, tn=TN_WIDE, name="ret_in")
    if L % RET_CHUNK == 0:
        C, c_real, Lp = RET_CHUNK, RET_CHUNK, L
        cos_t, sin_t = _rotation_tables(pos0 + jnp.arange(L, dtype=jnp.int32), dk)
        y, s_new = retention(qkvg, s0, w["gn_g"], cos_t, sin_t, B=B, L=L, H=RET_HEADS, C=C, c_real=c_real)
    else:
        C, c_real, Lp = SAMPLE_ROWS, L, SAMPLE_ROWS
        assert L == 1
        cos_t, sin_t = _rotation_tables(pos0 + jnp.arange(Lp, dtype=jnp.int32), dk)
        padded = jnp.pad(qkvg[:B].reshape(B, L, 6 * D), ((0, 0), (0, Lp - L), (0, 0))).reshape(B * Lp, 6 * D)
        y, s_new = retention(padded, s0, w["gn_g"], cos_t, sin_t, B=B, L=Lp, H=RET_HEADS, C=C, c_real=c_real)
        y = jnp.pad(y.reshape(B, Lp, 2 * D)[:, 0], ((0, M - B), (0, 0)))
    pre1 = matmul(y, [(w["o"], 0)], n=D, tm=min(tm, 512), tn=512, extras=(x32,),
                  epi=lambda acc, xt: ALPHA * xt + acc, name="ret_o")
    return pre1, s_new


def _rwkv_layer(x32, x_prev, w, s0, *, M, B, L):
    tm, tr = _tiles(M)
    D = x32.shape[1]
    xr, xk, xv, xw, xa, xg = shift_mix(x32, x_prev, w["mu"], tm=min(tr, 128))
    rkv, lead = w["rkv"]
    r = matmul(xr, [((rkv, lead + (0,)), 0)], n=D, tm=tm, tn=TN_WIDE, name="rwkv_r")
    k = matmul(xk, [((rkv, lead + (1,)), 0)], n=D, tm=tm, tn=TN_WIDE, name="rwkv_k")
    v = matmul(xv, [((rkv, lead + (2,)), 0)], n=D, tm=tm, tn=TN_WIDE, name="rwkv_v")
    lora = _mat_shape(w["w1"])[1]
    wmid = matmul(xw, [(w["w1"], 0)], n=lora, tm=tm, tn=lora, epi=jnp.tanh, out_dtypes=(BF16,), name="rwkv_w1")

    def log_decay(acc, w0):
        z = -(w0 + acc)
        softplus = jnp.maximum(z, 0.0) + jnp.log1p(jnp.exp(-jnp.abs(z)))
        return -jnp.exp(-softplus - 0.5)

    lw = matmul(wmid, [(w["w2"], 0)], n=D, tm=tm, tn=512, extras=((w["w0"], 0),), epi=log_decay, name="rwkv_w2")
    amid = matmul(xa, [(w["a1"], 0)], n=lora, tm=tm, tn=lora, out_dtypes=(BF16,), name="rwkv_a1")
    a = matmul(amid, [(w["a2"], 0)], n=D, tm=tm, tn=512, extras=((w["a0"], 0),),
               epi=lambda acc, a0: jax.nn.sigmoid(a0 + acc), name="rwkv_a2")
    gl = _mat_shape(w["g1"])[1]
    gmid = matmul(xg, [(w["g1"], 0)], n=gl, tm=tm, tn=gl, epi=jax.nn.sigmoid, out_dtypes=(BF16,), name="rwkv_g1")
    g = matmul(gmid, [(w["g2"], 0)], n=D, tm=tm, tn=512, name="rwkv_g2")
    params = (w["k_k"], w["k_a"], w["r_k"], w["gn_g"], w["gn_b"])
    if L % RWKV_CHUNK == 0:
        y, s_new = rwkv_recurrence(r, k, v, lw, a, g, *params, s0, B=B, L=L, C=RWKV_CHUNK)
    else:
        assert L == 1
        Lp = SAMPLE_ROWS
        pad = lambda t: jnp.pad(t[:B].reshape(B, L, D), ((0, 0), (0, Lp - L), (0, 0))).reshape(B * Lp, D)
        y, s_new = rwkv_recurrence(*[pad(t) for t in (r, k, v, lw, a, g)], *params, s0, B=B, L=Lp, C=Lp)
        y = jnp.pad(y.reshape(B, Lp, D)[:, 0], ((0, M - B), (0, 0)))
    pre1 = matmul(y, [(w["o"], 0)], n=D, tm=tm, tn=512, extras=(x32,),
                  epi=lambda acc, xt: ALPHA * xt + acc, name="rwkv_o")
    return pre1, s_new


def kernel(x_prompt, x_sample, p_prompt, p_sample, cache_moba_k, cache_moba_v, page_table, state_ret, state_wkv, state_shift, ln1_g, ln1_b, ln2_g, ln2_b, w_ffn_in, w_ffn_out, w_ple_gate, w_ple_proj, rel_bias, a_w_qkv, a_w_o, r_w_in, r_gn_g, r_w_o, c_mu, c_w_rkv, c_w0, c_w1, c_w2, c_a0, c_a1, c_a2, c_g1, c_g2, c_k_k, c_k_a, c_r_k, c_gn_g, c_gn_b, c_w_o):
    Bp, Sp, D = x_prompt.shape
    Bs, Ts, _ = x_sample.shape
    assert Ts == 1 and Bs <= SAMPLE_ROWS
    Mp, Ms = Bp * Sp, SAMPLE_ROWS
    past = page_table.shape[1] * PAGE_SIZE
    bf = lambda t: t.astype(BF16)
    pad_rows = lambda t: jnp.pad(t, ((0, Ms - Bs), (0, 0)))

    xp32 = x_prompt.reshape(Mp, D)
    xs32 = pad_rows(x_sample.reshape(Bs, D))
    xp16, xs16 = bf(xp32), bf(xs32)
    k_pool, v_pool = cache_moba_k, cache_moba_v
    gpad = W_GATE_LORA_PAD - c_g1.shape[-1]
    w16 = dict(ffn_in=bf(w_ffn_in), ffn_out=bf(w_ffn_out), ple_gate=bf(w_ple_gate), ple_proj=bf(w_ple_proj),
               qkv=bf(a_w_qkv), a_o=bf(a_w_o), r_in=bf(r_w_in), r_o=bf(r_w_o), rkv=bf(c_w_rkv), w1=bf(c_w1),
               w2=bf(c_w2), a1=bf(c_a1), a2=bf(c_a2), g1=bf(jnp.pad(c_g1, ((0, 0), (0, 0), (0, gpad)))),
               g2=bf(jnp.pad(c_g2, ((0, 0), (0, gpad), (0, 0)))), c_o=bf(c_w_o))

    kp_l, vp_l, ks_l, vs_l, rp_l, rs_l, wp_l, ws_l, hp_l, hs_l = ([] for _ in range(10))
    for i in range(DEPTH):
        kind, j = i % 3, i // 3
        tail = dict(ln1_g=ln1_g[i], ln1_b=ln1_b[i], ln2_g=ln2_g[i], ln2_b=ln2_b[i],
                    ffn_in=(w16["ffn_in"], (i,)), ffn_out=(w16["ffn_out"], (i,)),
                    ple_gate=(w16["ple_gate"], (i,)), ple_proj=(w16["ple_proj"], (i,)))
        if kind == 0:
            w = dict(qkv=(w16["qkv"], (j,)), o=(w16["a_o"], (j,)))
            pre_p, kp, vp = _moba_layer(xp32, xp16, w, rel_bias, M=Mp, prompt=True, B=Bp, S=Sp)
            pre_s, ks, vs = _moba_layer(xs32, xs16, w, rel_bias, M=Ms, prompt=False, B=Bs, k_pool=k_pool,
                                        v_pool=v_pool, layer=j, page_table=page_table)
            kp_l.append(kp.reshape(Bp, Sp, A_HEADS, D // A_HEADS))
            vp_l.append(vp.reshape(Bp, Sp, A_HEADS, D // A_HEADS))
            ks_l.append(ks[:Bs].reshape(Bs, Ts, A_HEADS, D // A_HEADS))
            vs_l.append(vs[:Bs].reshape(Bs, Ts, A_HEADS, D // A_HEADS))
        elif kind == 1:
            w = dict(o=(w16["r_o"], (j,)), gn_g=r_gn_g[j])
            w["in"] = (w16["r_in"], (j,))
            s0 = jnp.zeros((Bp,) + state_ret.shape[2:], state_ret.dtype)
            pre_p, sp = _ret_layer(xp32, xp16, w, s0, 0, M=Mp, B=Bp, L=Sp)
            pre_s, ss = _ret_layer(xs32, xs16, w, state_ret[j], past, M=Ms, B=Bs, L=Ts)
            rp_l.append(sp)
            rs_l.append(ss)
        else:
            w = dict(mu=c_mu[j], w0=c_w0[j].reshape(1, D), a0=c_a0[j].reshape(1, D),
                     k_k=c_k_k[j], k_a=c_k_a[j], r_k=c_r_k[j], gn_g=c_gn_g[j], gn_b=c_gn_b[j], o=(w16["c_o"], (j,)))
            w.update({name: (w16[name], (j,)) for name in ("rkv", "w1", "w2", "a1", "a2", "g1", "g2")})
            xp3 = xp32.reshape(Bp, Sp, D)
            xprev_p = jnp.concatenate([jnp.zeros((Bp, 1, D), F32), xp3[:, :-1]], axis=1).reshape(Mp, D)
            xprev_s = pad_rows(state_shift[j])
            wk0 = jnp.zeros((Bp,) + state_wkv.shape[2:], state_wkv.dtype)
            pre_p, wkp = _rwkv_layer(xp32, xprev_p, w, wk0, M=Mp, B=Bp, L=Sp)
            pre_s, wks = _rwkv_layer(xs32, xprev_s, w, state_wkv[j], M=Ms, B=Bs, L=Ts)
            wp_l.append(wkp)
            ws_l.append(wks)
            hp_l.append(xp3[:, -1])
            hs_l.append(xs32[:Bs])
        xp32, xp16 = _layer_tail(pre_p, bf(p_prompt[i].reshape(Mp, -1)), tail, M=Mp)
        xs32, xs16 = _layer_tail(pre_s, bf(pad_rows(p_sample[i].reshape(Bs, -1))), tail, M=Ms)
    return (xp32.reshape(Bp, Sp, D), xs32[:Bs].reshape(Bs, Ts, D),
            jnp.stack(kp_l), jnp.stack(vp_l), jnp.stack(ks_l), jnp.stack(vs_l),
            jnp.stack(rp_l), jnp.stack(rs_l), jnp.stack(wp_l), jnp.stack(ws_l),
            jnp.stack(hp_l), jnp.stack(hs_l))
```

```python
import functools
import math

import numpy as np
import jax
import jax.numpy as jnp
from jax import lax
from jax.experimental import pallas as pl
from jax.experimental.pallas import tpu as pltpu

F32 = jnp.float32
BF16 = jnp.bfloat16

DEPTH = 4
LN_EPS = 1e-5
ALPHA = (2 * DEPTH) ** 0.25
A_HEADS = 32
MOBA_BLOCK = 256
MOBA_TOPK = 3
PAGE_SIZE = 128
REL_BUCKETS = 32
REL_MAX_DIST = 128
RET_HEADS = 16
RET_CHUNK = 128
RET_GN_EPS = 1e-5
W_HEAD = 64
W_GN_EPS = 64e-5
W_GATE_LORA_PAD = 512
NEG_INF = -1e30
LOG2E = math.log2(math.e)

V7X_VMEM_LIMIT_BYTES = 56 * 1024 * 1024
LANES = 128

MOBA_HEADS_PER_STEP = 4
RET_HEADS_PER_STEP = 8
RWKV_CHUNK = 64
RWKV_HEADS_PER_STEP = 32
SAMPLE_ROWS = 16


def _t5_bucket_thresholds():
    n = np.arange(0, 4 * REL_MAX_DIST)
    max_exact = REL_BUCKETS // 2
    nf = np.maximum(n, 1).astype(np.float32)
    large = max_exact + (np.log(nf / np.float32(max_exact)) / np.float32(math.log(REL_MAX_DIST / max_exact))
                         * np.float32(REL_BUCKETS - max_exact)).astype(np.int32)
    bucket = np.where(n < max_exact, n, np.minimum(large, REL_BUCKETS - 1))
    assert sorted(set(bucket.tolist())) == list(range(REL_BUCKETS))
    return tuple(int(np.argmax(bucket >= b)) for b in range(REL_BUCKETS))


T5_THRESHOLDS = _t5_bucket_thresholds()


def _params(*sem):
    return pltpu.CompilerParams(dimension_semantics=sem, vmem_limit_bytes=V7X_VMEM_LIMIT_BYTES)


def _bias_from_distance(dist, table_ref, h):
    t = jnp.full(dist.shape, table_ref[0, h], F32)
    for b in range(1, REL_BUCKETS):
        t = jnp.where(dist >= T5_THRESHOLDS[b], table_ref[b, h], t)
    return t


def _mm_kernel(epi, n_w, n_ex, x_ref, *refs):
    w_refs = refs[:n_w]
    ex_refs = refs[n_w:n_w + n_ex]
    o_refs = refs[n_w + n_ex:]
    x = x_ref[...]
    accs = [jnp.dot(x, w[...], preferred_element_type=F32) for w in w_refs]
    outs = epi(*accs, *[e[...] for e in ex_refs])
    if not isinstance(outs, tuple):
        outs = (outs,)
    for o_ref, o in zip(o_refs, outs):
        o_ref[...] = o.astype(o_ref.dtype)


def matmul(x, ws, *, n, tm, tn, epi=None, extras=(), out_dtypes=(F32,), x_buffers=2, name="mm"):
    M, K = x.shape
    assert M % tm == 0 and n % tn == 0, (M, tm, n, tn)
    if epi is None:
        epi = lambda a: a
    in_specs = [pl.BlockSpec((tm, K), lambda i, j: (i, 0), pipeline_mode=pl.Buffered(x_buffers))]
    args = [x]
    for w, col0 in ws:
        w, lead = w if isinstance(w, tuple) else (w, ())
        assert w.shape[-2] == K and col0 % tn == 0 and w.ndim == len(lead) + 2
        in_specs.append(pl.BlockSpec((None,) * len(lead) + (K, tn),
                                     lambda i, j, cb=col0 // tn, lead=lead: lead + (0, j + cb)))
        args.append(w)
    for e in extras:
        if isinstance(e, tuple):
            row, col0 = e
            assert row.shape[0] == 1 and col0 % tn == 0
            in_specs.append(pl.BlockSpec((1, tn), lambda i, j, cb=col0 // tn: (0, j + cb)))
            args.append(row)
        else:
            assert e.shape == (M, n)
            in_specs.append(pl.BlockSpec((tm, tn), lambda i, j: (i, j)))
            args.append(e)
    out_shape = tuple(jax.ShapeDtypeStruct((M, n), dt) for dt in out_dtypes)
    out_specs = tuple(pl.BlockSpec((tm, tn), lambda i, j: (i, j)) for _ in out_dtypes)
    res = pl.pallas_call(
        functools.partial(_mm_kernel, epi, len(ws), len(extras)),
        grid=(M // tm, n // tn),
        in_specs=in_specs,
        out_specs=out_specs,
        out_shape=out_shape,
        compiler_params=_params("arbitrary", "arbitrary"),
        name=name,
    )(*args)
    return res if len(out_dtypes) > 1 else res[0]


def _ln_kernel(x_ref, g_ref, b_ref, o32_ref, o16_ref):
    x = x_ref[...]
    mu = jnp.mean(x, axis=-1, keepdims=True)
    xc = x - mu
    var = jnp.mean(xc * xc, axis=-1, keepdims=True)
    y = xc * lax.rsqrt(var + LN_EPS) * g_ref[...] + b_ref[...]
    o32_ref[...] = y
    o16_ref[...] = y.astype(BF16)


def layer_norm(x, g, b, *, tm):
    M, D = x.shape
    row = pl.BlockSpec((1, D), lambda i: (0, 0))
    tile = pl.BlockSpec((tm, D), lambda i: (i, 0))
    return pl.pallas_call(
        _ln_kernel,
        grid=(M // tm,),
        in_specs=[tile, row, row],
        out_specs=(tile, tile),
        out_shape=(jax.ShapeDtypeStruct((M, D), F32), jax.ShapeDtypeStruct((M, D), BF16)),
        compiler_params=_params("arbitrary"),
        name="layer_norm",
    )(x, g.reshape(1, D), b.reshape(1, D))


def _moba_prompt_kernel(nb, G, tab_ref, q_ref, k_ref, v_ref, o_ref, bown, bprev, kmean, vt, m_s, l_s, acc_s):
    hg = pl.program_id(0)
    b = pl.program_id(1)
    qb = pl.program_id(2)
    blk = MOBA_BLOCK
    n_cand = nb - 1
    k_sel_n = min(MOBA_TOPK, n_cand)
    hd = q_ref.shape[1] // G
    score_scale = hd ** -0.5 * LOG2E
    h0 = hg * G
    key = lax.broadcasted_iota(jnp.int32, (blk, blk), 0)
    qry = lax.broadcasted_iota(jnp.int32, (blk, blk), 1)
    heads = lambda x: jnp.stack([x[:, g * hd:(g + 1) * hd] for g in range(G)])

    @pl.when((b == 0) & (qb == 0))
    def _():
        for g in range(G):
            bown[g] = _bias_from_distance(jnp.maximum(qry - key, 0), tab_ref, h0 + g) * LOG2E
            bprev[g] = _bias_from_distance(qry - key + blk, tab_ref, h0 + g) * LOG2E

    @pl.when(qb == 0)
    def _():
        kmean[...] = jnp.zeros(kmean.shape, F32)
        for n in range(n_cand):
            mean_n = jnp.mean(k_ref[n * blk:(n + 1) * blk, :], axis=0, keepdims=True)
            for g in range(G):
                kmean[g, n:n + 1, :] = mean_n[:, g * hd:(g + 1) * hd]
        for n in range(nb):
            for g in range(G):
                vt[n, g] = v_ref[n * blk:(n + 1) * blk, g * hd:(g + 1) * hd].T.astype(BF16)

    q = heads(q_ref[...])
    q16 = q.astype(BF16)
    dn_kq = (((2,), (2,)), ((0,), (0,)))
    dn_vp = (((2,), (1,)), ((0,), (0,)))

    start = pl.multiple_of(qb * blk, blk)
    k_own = heads(k_ref[pl.ds(start, blk), :]).astype(BF16)
    s = lax.dot_general(k_own, q16, dn_kq, preferred_element_type=F32) * score_scale + bown[...]
    s = jnp.where(qry >= key, s, NEG_INF)
    m0 = jnp.max(s, axis=1, keepdims=True)
    p = jnp.exp2(s - m0)
    m_s[...] = m0
    l_s[...] = jnp.sum(p, axis=1, keepdims=True)
    acc_s[...] = lax.dot_general(vt[qb], p.astype(BF16), dn_vp, preferred_element_type=F32)

    if n_cand > 0:
        ncols = kmean.shape[1]
        gate = lax.dot_general(kmean[...], q, dn_kq, precision=lax.Precision.HIGHEST,
                               preferred_element_type=F32)
        cand = lax.broadcasted_iota(jnp.int32, (G, ncols, blk), 1)
        gate = jnp.where(cand < qb, gate, NEG_INF)
        rank = jnp.zeros((G, ncols, blk), F32)
        for m in range(n_cand):
            gm = gate[:, m:m + 1, :]
            ahead = (gm > gate) | ((gm == gate) & (cand > m))
            rank = rank + jnp.where(ahead, 1.0, 0.0)
        sel = jnp.where((rank < k_sel_n) & (cand < qb), 1.0, 0.0)
        gid = lax.broadcasted_iota(jnp.int32, (G, 1, 1), 0)
        far_bias = jnp.full((G, 1, 1), tab_ref[REL_BUCKETS - 1, h0], F32)
        for g in range(1, G):
            far_bias = jnp.where(gid == g, tab_ref[REL_BUCKETS - 1, h0 + g], far_bias)
        far_bias = far_bias * LOG2E

        def attend(n, just_before):
            kn = heads(k_ref[n * blk:(n + 1) * blk, :]).astype(BF16)
            bias = bprev[...] if just_before else far_bias
            sn = lax.dot_general(kn, q16, dn_kq, preferred_element_type=F32) * score_scale + bias
            sn = jnp.where(sel[:, n:n + 1, :] > 0.5, sn, NEG_INF)
            m_old = m_s[...]
            m_new = jnp.maximum(m_old, jnp.max(sn, axis=1, keepdims=True))
            scale_old = jnp.exp2(m_old - m_new)
            pn = jnp.exp2(sn - m_new)
            l_s[...] = scale_old * l_s[...] + jnp.sum(pn, axis=1, keepdims=True)
            acc_s[...] = scale_old * acc_s[...] + lax.dot_general(vt[n], pn.astype(BF16), dn_vp,
                                                                  preferred_element_type=F32)
            m_s[...] = m_new

        for n in range(n_cand):
            pl.when(n == qb - 1)(functools.partial(attend, n, True))
            pl.when(n < qb - 1)(functools.partial(attend, n, False))

    out = acc_s[...] / l_s[...]
    for g in range(G):
        o_ref[:, g * hd:(g + 1) * hd] = out[g].T.astype(o_ref.dtype)


def moba_prompt(q, k, v, rel_bias, *, B, S, H):
    M, D = q.shape
    hd = D // H
    blk = MOBA_BLOCK
    G = math.gcd(H, MOBA_HEADS_PER_STEP)
    assert S % blk == 0 and M == B * S and T5_THRESHOLDS[-1] <= blk
    nb = S // blk
    ncols = max(8, -(-(nb - 1) // 8) * 8)
    qspec = pl.BlockSpec((blk, G * hd), lambda hg, b, qb: (b * nb + qb, hg))
    kvspec = pl.BlockSpec((S, G * hd), lambda hg, b, qb: (b, hg))
    return pl.pallas_call(
        functools.partial(_moba_prompt_kernel, nb, G),
        grid=(H // G, B, nb),
        in_specs=[pl.BlockSpec(memory_space=pltpu.SMEM), qspec, kvspec, kvspec],
        out_specs=qspec,
        out_shape=jax.ShapeDtypeStruct((M, D), BF16),
        scratch_shapes=[pltpu.VMEM((G, blk, blk), F32), pltpu.VMEM((G, blk, blk), F32),
                        pltpu.VMEM((G, ncols, hd), F32), pltpu.VMEM((nb, G, hd, blk), BF16),
                        pltpu.VMEM((G, 1, blk), F32), pltpu.VMEM((G, 1, blk), F32),
                        pltpu.VMEM((G, hd, blk), F32)],
        compiler_params=_params("arbitrary", "arbitrary", "arbitrary"),
        name="moba_prompt",
    )(rel_bias, q, k, v)


def _moba_select_kernel(n_blocks, pt_ref, pa_ref, pb_ref, q_ref, sel_ref, sums):
    j = pl.program_id(1)
    sums[j] = jnp.sum(pa_ref[0, 0], axis=0) + jnp.sum(pb_ref[0, 0], axis=0)

    @pl.when(j == n_blocks - 1)
    def _():
        prod = (sums[...] * (1.0 / MOBA_BLOCK)) * q_ref[...]
        gate = jnp.sum(prod, axis=-1, keepdims=True)
        blk_id = lax.broadcasted_iota(jnp.int32, gate.shape, 0)
        rank = jnp.zeros(gate.shape, F32)
        for m in range(n_blocks):
            gm = gate[m:m + 1]
            ahead = (gm > gate) | ((gm == gate) & (blk_id > m))
            rank = rank + jnp.where(ahead, 1.0, 0.0)
        for r in range(sel_ref.shape[1]):
            sel_ref[0, r] = jnp.sum(jnp.where(rank == float(r), blk_id, 0), axis=0)


def _moba_sample_attn_kernel(n_sel_pages, past, layer, n_pages, pt_ref, sel_ref, tab_ref, q_ref, kn_ref, vn_ref,
                             k_hbm, v_hbm, o_ref, kbuf, vbuf, sem):
    b = pl.program_id(0)
    h = pl.program_id(1)
    n_heads = pl.num_programs(1)
    step = b * n_heads + h
    n_steps = pl.num_programs(0) * n_heads
    hd = q_ref.shape[-1]
    a_scale = hd ** -0.5
    pages_per_block = MOBA_BLOCK // PAGE_SIZE

    def page_copies(s, slot):
        bb = lax.div(s, n_heads)
        hh = lax.rem(s, n_heads)
        out = []
        for t in range(n_sel_pages):
            blk = sel_ref[s * MOBA_TOPK + t // pages_per_block]
            page = pt_ref[bb * n_pages + pages_per_block * blk + t % pages_per_block]
            out.append(pltpu.make_async_copy(k_hbm.at[layer, page, :, hh, :], kbuf.at[slot, t], sem.at[0, slot, t]))
            out.append(pltpu.make_async_copy(v_hbm.at[layer, page, :, hh, :], vbuf.at[slot, t], sem.at[1, slot, t]))
        return out

    slot = lax.rem(step, 2)

    @pl.when(step == 0)
    def _():
        for c in page_copies(step, slot):
            c.start()

    @pl.when(step + 1 < n_steps)
    def _():
        for c in page_copies(step + 1, 1 - slot):
            c.start()

    for c in page_copies(step, slot):
        c.wait()

    q = q_ref[0]
    s_own = jnp.sum(q * kn_ref[0], axis=-1, keepdims=True) * a_scale + tab_ref[0, h]
    r = lax.broadcasted_iota(jnp.int32, (PAGE_SIZE, 1), 0)
    scores = []
    for t in range(n_sel_pages):
        blk = sel_ref[step * MOBA_TOPK + t // pages_per_block]
        pos0 = blk * MOBA_BLOCK + (t % pages_per_block) * PAGE_SIZE
        dist = past - (pos0 + r)
        bias = _bias_from_distance(dist, tab_ref, h)
        scores.append(jnp.sum(kbuf[slot, t] * q, axis=-1, keepdims=True) * a_scale + bias)
    m = s_own
    for s in scores:
        m = jnp.maximum(m, jnp.max(s, axis=0, keepdims=True))
    p_own = jnp.exp(s_own - m)
    l = p_own
    acc = p_own * vn_ref[0]
    for t, s in enumerate(scores):
        p = jnp.exp(s - m)
        l = l + jnp.sum(p, axis=0, keepdims=True)
        acc = acc + jnp.sum(p * vbuf[slot, t], axis=0, keepdims=True)
    o_ref[0] = (acc / l).astype(o_ref.dtype)


def moba_sample(q, k_new, v_new, k_pool, v_pool, layer, page_table, rel_bias, *, H):
    rows, D = q.shape
    B, n_pages = page_table.shape
    assert rows == B
    hd = D // H
    past = n_pages * PAGE_SIZE
    ppb = MOBA_BLOCK // PAGE_SIZE
    assert past % MOBA_BLOCK == 0
    n_blocks = past // MOBA_BLOCK
    assert n_blocks >= MOBA_TOPK
    pt = page_table.reshape(-1)
    q3 = q.reshape(rows, 1, D)

    page = lambda half: pl.BlockSpec(
        (1, 1, PAGE_SIZE, H, hd), lambda b, j, pt_ref: (layer, pt_ref[b * n_pages + ppb * j + half], 0, 0, 0))
    sel = pl.pallas_call(
        functools.partial(_moba_select_kernel, n_blocks),
        grid_spec=pltpu.PrefetchScalarGridSpec(
            num_scalar_prefetch=1,
            grid=(B, n_blocks),
            in_specs=[page(0), page(1), pl.BlockSpec((1, H, hd), lambda b, j, pt_ref: (b, 0, 0))],
            out_specs=pl.BlockSpec((1, MOBA_TOPK, H, 1), lambda b, j, pt_ref: (b, 0, 0, 0)),
            scratch_shapes=[pltpu.VMEM((n_blocks, H, hd), F32)]),
        out_shape=jax.ShapeDtypeStruct((B, MOBA_TOPK, H, 1), jnp.int32),
        compiler_params=_params("arbitrary", "arbitrary"),
        name="moba_sample_select",
    )(pt, k_pool, k_pool, q.reshape(B, H, hd))
    sel_flat = jnp.transpose(sel[..., 0], (0, 2, 1)).reshape(-1)

    n_sel_pages = MOBA_TOPK * ppb
    tok = pl.BlockSpec((1, 1, hd), lambda b, h, pt_ref, sel_ref: (b, 0, h))
    hbm = pl.BlockSpec(memory_space=pl.ANY)
    out = pl.pallas_call(
        functools.partial(_moba_sample_attn_kernel, n_sel_pages, past, layer, n_pages),
        grid_spec=pltpu.PrefetchScalarGridSpec(
            num_scalar_prefetch=2,
            grid=(B, H),
            in_specs=[pl.BlockSpec(memory_space=pltpu.SMEM), tok, tok, tok, hbm, hbm],
            out_specs=tok,
            scratch_shapes=[pltpu.VMEM((2, n_sel_pages, PAGE_SIZE, hd), F32),
                            pltpu.VMEM((2, n_sel_pages, PAGE_SIZE, hd), F32),
                            pltpu.SemaphoreType.DMA((2, 2, n_sel_pages))]),
        out_shape=jax.ShapeDtypeStruct((rows, 1, D), BF16),
        compiler_params=_params("arbitrary", "arbitrary"),
        name="moba_sample_attn",
    )(pt, sel_flat, rel_bias, q3, k_new.reshape(rows, 1, D), v_new.reshape(rows, 1, D), k_pool, v_pool)
    return out.reshape(rows, D)


def _retention_kernel(c_real, lg_ref, q_ref, k_ref, v_ref, g_ref, cos_ref, sin_ref, gn_ref, s0_ref,
                      y_ref, s_out_ref, s_acc):
    hg = pl.program_id(1)
    c = pl.program_id(2)
    n_chunks = pl.num_programs(2)
    C, dk = cos_ref.shape
    G = q_ref.shape[1] // dk
    dv = v_ref.shape[1] // G

    @pl.when(c == 0)
    def _():
        s_acc[...] = s0_ref[0]

    gid = lax.broadcasted_iota(jnp.int32, (G, 1, 1), 0)
    lg = jnp.full((G, 1, 1), lg_ref[hg * G], F32)
    for g in range(1, G):
        lg = jnp.where(gid == g, lg_ref[hg * G + g], lg)
    cos = cos_ref[...]
    sin = sin_ref[...]
    even = (lax.broadcasted_iota(jnp.int32, (C, LANES), 1) % 2) == 0
    heads = lambda x, w: jnp.stack([x[:, g * w:(g + 1) * w] for g in range(G)])

    def rotate(x):
        tiles = []
        for t in range(dk // LANES):
            xt = x[:, t * LANES:(t + 1) * LANES]
            tiles.append(jnp.where(even, pltpu.roll(xt, LANES - 1, 1), pltpu.roll(xt, 1, 1)))
        return x * cos + jnp.concatenate(tiles, axis=1) * sin

    q = jnp.stack([rotate(q_ref[:, g * dk:(g + 1) * dk]) for g in range(G)])
    k = jnp.stack([rotate(k_ref[:, g * dk:(g + 1) * dk]) for g in range(G)]) * (dk ** -0.5)
    v16 = heads(v_ref[...], dv).astype(BF16)
    idx = lax.broadcasted_iota(jnp.int32, (1, C, 1), 1).astype(F32)
    q_dec = jnp.exp((idx + 1.0) * lg)
    k_dec = jnp.exp((c_real - 1.0 - idx) * lg)
    ii = lax.broadcasted_iota(jnp.int32, (1, C, C), 1)
    jj = lax.broadcasted_iota(jnp.int32, (1, C, C), 2)
    diff = (ii - jj).astype(F32)
    dmask = jnp.where(diff >= 0, jnp.exp(jnp.maximum(diff, 0.0) * lg), 0.0)
    dg = lambda a, b, dn: lax.dot_general(a.astype(BF16), b.astype(BF16), dn, preferred_element_type=F32)
    qk = dg(q, k, _DN_NT) * dmask
    s_prev = s_acc[...]
    o = dg(qk, v16, _DN_NN) + dg(q * q_dec, s_prev, _DN_NN)
    kv = dg(k * k_dec, v16, _DN_TN)
    c_dec = jnp.exp(c_real * lg)
    s_new = s_prev * c_dec + kv
    s_acc[...] = s_new

    mu = jnp.mean(o, axis=-1, keepdims=True)
    oc = o - mu
    var = jnp.mean(oc * oc, axis=-1, keepdims=True)
    gate = heads(g_ref[...], dv)
    y = oc * lax.rsqrt(var + RET_GN_EPS) * heads(gn_ref[...], dv) * (gate * jax.nn.sigmoid(gate))
    for g in range(G):
        y_ref[:, g * dv:(g + 1) * dv] = y[g].astype(y_ref.dtype)

    @pl.when(c == n_chunks - 1)
    def _():
        s_out_ref[0] = s_new


def retention(qkvg, s0, gn_g, cos_t, sin_t, *, B, L, H, C, c_real):
    M, N6 = qkvg.shape
    D = N6 // 6
    dk, dv = D // H, 2 * D // H
    assert L % C == 0 and M == B * L
    nC = L // C
    G = math.gcd(H, RET_HEADS_PER_STEP)
    nG = H // G
    log_g = jnp.log1p(-jnp.exp2(-5.0 - jnp.arange(H, dtype=F32)))
    tokspec = lambda width, sec: pl.BlockSpec((C, G * width), lambda b, hg, c: (b * nC + c, sec * nG + hg))
    tabspec = pl.BlockSpec((C, dk), lambda b, hg, c: (c, 0))
    sspec = pl.BlockSpec((1, G, dk, dv), lambda b, hg, c: (b, hg, 0, 0))
    return pl.pallas_call(
        functools.partial(_retention_kernel, float(c_real)),
        grid=(B, nG, nC),
        in_specs=[pl.BlockSpec(memory_space=pltpu.SMEM),
                  tokspec(dk, 0), tokspec(dk, 1), tokspec(dv, 1), tokspec(dv, 2),
                  tabspec, tabspec, pl.BlockSpec((1, G * dv), lambda b, hg, c: (0, hg)), sspec],
        out_specs=(pl.BlockSpec((C, G * dv), lambda b, hg, c: (b * nC + c, hg)), sspec),
        out_shape=(jax.ShapeDtypeStruct((M, 2 * D), BF16), jax.ShapeDtypeStruct(s0.shape, F32)),
        scratch_shapes=[pltpu.VMEM((G, dk, dv), F32)],
        compiler_params=_params("arbitrary", "arbitrary", "arbitrary"),
        name="retention",
    )(log_g, qkvg, qkvg, qkvg, qkvg, cos_t, sin_t, gn_g.reshape(1, 2 * D), s0)


def _rotation_tables(pos, dk):
    inv = 1.0 / (10000.0 ** jnp.linspace(0.0, 1.0, dk // 2, dtype=F32))
    ang = pos.astype(F32)[:, None] * inv
    cos = jnp.repeat(jnp.cos(ang), 2, axis=1)
    sin = jnp.stack([-jnp.sin(ang), jnp.sin(ang)], axis=-1).reshape(pos.shape[0], dk)
    return cos, sin


def _shift_mix_kernel(x_ref, xp_ref, mu_ref, *o_refs):
    x = x_ref[...]
    xx = xp_ref[...] - x
    for i, o_ref in enumerate(o_refs):
        o_ref[...] = (x + xx * mu_ref[i:i + 1, :]).astype(o_ref.dtype)


def shift_mix(x, x_prev, mu, *, tm):
    M, D = x.shape
    n = mu.shape[0]
    tile = pl.BlockSpec((tm, D), lambda i: (i, 0))
    return pl.pallas_call(
        _shift_mix_kernel,
        grid=(M // tm,),
        in_specs=[tile, tile, pl.BlockSpec((n, D), lambda i: (0, 0))],
        out_specs=tuple(tile for _ in range(n)),
        out_shape=tuple(jax.ShapeDtypeStruct((M, D), BF16) for _ in range(n)),
        compiler_params=_params("arbitrary"),
        name="rwkv_shift_mix",
    )(x, x_prev, mu)


def _split3(x):
    hi = x.astype(BF16)
    r1 = x - hi.astype(F32)
    mid = r1.astype(BF16)
    lo = (r1 - mid.astype(F32)).astype(BF16)
    return hi, mid, lo


def _dot16(a, b, dn):
    return lax.dot_general(a.astype(BF16), b.astype(BF16), dimension_numbers=dn, preferred_element_type=F32)


_DN_NN = (((2,), (1,)), ((0,), (0,)))
_DN_NT = (((2,), (2,)), ((0,), (0,)))
_DN_TN = (((1,), (1,)), ((0,), (0,)))


def _rwkv_kernel(r_ref, k_ref, v_ref, lw_ref, a_ref, g_ref, kk_ref, ka_ref, rk_ref, gng_ref, gnb_ref, s0_ref,
                 y_ref, s_out_ref, s_acc):
    c = pl.program_id(2)
    n_chunks = pl.num_programs(2)
    C = r_ref.shape[0]
    N = W_HEAD
    G = r_ref.shape[1] // N

    @pl.when(c == 0)
    def _():
        s_acc[...] = s0_ref[0]

    ti = lax.broadcasted_iota(jnp.int32, (C, C), 0)
    tj = lax.broadcasted_iota(jnp.int32, (C, C), 1)
    strict = ti > tj
    incl = ti >= tj
    tri16 = jnp.broadcast_to(jnp.where(incl, 1.0, 0.0).astype(BF16), (G, C, C))
    n_double = max(1, (C - 1).bit_length())
    ti2 = lax.broadcasted_iota(jnp.int32, (C, 2 * C), 0)
    tj2 = lax.broadcasted_iota(jnp.int32, (C, 2 * C), 1)
    k_cols = tj2 >= C
    tj2 = jnp.where(k_cols, tj2 - C, tj2)
    strict_k = (ti2 > tj2) & k_cols
    incl2 = ti2 >= tj2

    heads = lambda ref: jnp.stack([ref[:, hh * N:(hh + 1) * N] for hh in range(G)])
    r, k, v, lw, a = heads(r_ref), heads(k_ref), heads(v_ref), heads(lw_ref), heads(a_ref)
    kk = k * heads(kk_ref)
    kk = kk / jnp.maximum(jnp.sqrt(jnp.sum(kk * kk, axis=-1, keepdims=True)), 1e-12)
    kh = k * (1.0 + (a - 1.0) * heads(ka_ref))
    av = -kk
    bv = kk * a
    dg = functools.partial(lax.dot_general, dimension_numbers=_DN_NN, preferred_element_type=F32)
    hi, mid, lo = _split3(lw)
    cum = dg(tri16, hi) + dg(tri16, mid) + dg(tri16, lo)
    e_in = jnp.exp(cum)
    e_ex = jnp.exp(cum - lw)
    e_neg = jnp.exp(-cum)
    at = av * e_ex
    rt = r * e_in
    bt = bv * e_neg
    kt = kh * e_neg
    s0 = s_acc[...]
    lh = jnp.concatenate([at, rt], axis=1)
    rh = jnp.concatenate([bt, kt], axis=1)
    amat = _dot16(lh, rh, _DN_NT)
    z = _dot16(lh, s0, _DN_NT)
    zero_v = jnp.zeros_like(v)
    a_ab = jnp.where(strict, amat[:, :C, :C], 0.0)
    a_ak = jnp.where(strict_k, amat[:, :C, :], 0.0)
    u = z[:, :C] + _dot16(a_ak, jnp.concatenate([zero_v, v], axis=1), _DN_NN)
    pw = a_ab
    for it in range(n_double):
        u = u + _dot16(pw, u, _DN_NN)
        if it + 1 < n_double:
            pw = _dot16(pw, pw, _DN_NN)
    uv = jnp.concatenate([u, v], axis=1)
    y = z[:, C:] + _dot16(jnp.where(incl2, amat[:, C:, :], 0.0), uv, _DN_NN)
    wc = e_in[:, C - 1:C, :]
    s_acc[...] = s0 * wc + _dot16(uv, rh * wc, _DN_TN)

    mu = jnp.mean(y, axis=-1, keepdims=True)
    yc = y - mu
    var = jnp.mean(yc * yc, axis=-1, keepdims=True)
    yn = yc * lax.rsqrt(var + W_GN_EPS)
    bonus = jnp.sum(r * kh * heads(rk_ref), axis=-1, keepdims=True) * v
    out = (yn * heads(gng_ref) + heads(gnb_ref) + bonus) * heads(g_ref)
    for hh in range(G):
        y_ref[:, hh * N:(hh + 1) * N] = out[hh].astype(y_ref.dtype)

    @pl.when(c == n_chunks - 1)
    def _():
        s_out_ref[0] = s_acc[...]


def rwkv_recurrence(r, k, v, lw, a, g, k_k, k_a, r_k, gn_g, gn_b, s0, *, B, L, C):
    M, D = r.shape
    N = W_HEAD
    H = D // N
    G = math.gcd(H, RWKV_HEADS_PER_STEP)
    assert L % C == 0 and M == B * L
    nC = L // C
    tok = pl.BlockSpec((C, G * N), lambda b, hg, c: (b * nC + c, hg))
    row = pl.BlockSpec((1, G * N), lambda b, hg, c: (0, hg))
    sspec = pl.BlockSpec((1, G, N, N), lambda b, hg, c: (b, hg, 0, 0))
    rows = [p.reshape(1, D) for p in (k_k, k_a, r_k, gn_g, gn_b)]
    return pl.pallas_call(
        _rwkv_kernel,
        grid=(B, H // G, nC),
        in_specs=[tok] * 6 + [row] * 5 + [sspec],
        out_specs=(tok, sspec),
        out_shape=(jax.ShapeDtypeStruct((M, D), BF16), jax.ShapeDtypeStruct(s0.shape, F32)),
        scratch_shapes=[pltpu.VMEM((G, N, N), F32)],
        compiler_params=_params("arbitrary", "arbitrary", "arbitrary"),
        name="rwkv_recurrence",
    )(r, k, v, lw, a, g, *rows, s0)


def _tiles(M):
    if M >= 1024:
        return 1024, 256
    return M, M


TN_WIDE = 1024
TN_FFN = 256


def _mat_shape(w):
    return (w[0] if isinstance(w, tuple) else w).shape[-2:]


def _layer_tail(pre1, p16, w, *, M):
    tm, tr = _tiles(M)
    F, D = _mat_shape(w["ffn_out"])
    h32, h16 = layer_norm(pre1, w["ln1_g"], w["ln1_b"], tm=tr)
    act = matmul(h16, [(w["ffn_in"], 0), (w["ffn_in"], F)], n=F, tm=min(M, 2 * tm), tn=TN_FFN,
                 epi=lambda gt, up: gt * jax.nn.sigmoid(gt) * up, out_dtypes=(BF16,), name="ffn_in")
    pp = matmul(p16, [(w["ple_proj"], 0)], n=D, tm=tm, tn=TN_WIDE, name="ple_proj")
    ple = matmul(h16, [(w["ple_gate"], 0)], n=D, tm=tm, tn=512, extras=(pp,),
                 epi=lambda acc, ppt: jax.nn.sigmoid(acc) * ppt, name="ple_gate")
    pre2 = matmul(act, [(w["ffn_out"], 0)], n=D, tm=tm, tn=TN_FFN, extras=(ple, h32), x_buffers=1,
                  epi=lambda acc, plet, ht: ALPHA * ht + acc + plet, name="ffn_out")
    return layer_norm(pre2, w["ln2_g"], w["ln2_b"], tm=tr)


def _moba_layer(x32, x16, w, rel_bias, *, M, prompt, B, S=None, k_pool=None, v_pool=None, layer=None,
                page_table=None):
    tm, _ = _tiles(M)
    D = x32.shape[1]
    q, k, v = [matmul(x16, [(w["qkv"], i * D)], n=D, tm=tm, tn=TN_WIDE, name="moba_qkv") for i in range(3)]
    if prompt:
        att = moba_prompt(q, k, v, rel_bias, B=B, S=S, H=A_HEADS)
    else:
        att = moba_sample(q[:B], k[:B], v[:B], k_pool, v_pool, layer, page_table, rel_bias, H=A_HEADS)
        att = jnp.pad(att, ((0, M - B), (0, 0)))
    pre1 = matmul(att, [(w["o"], 0)], n=D, tm=tm, tn=512, extras=(x32,),
                  epi=lambda acc, xt: ALPHA * xt + acc, name="moba_o")
    return pre1, k, v


def _ret_layer(x32, x16, w, s0, pos0, *, M, B, L):
    tm, _ = _tiles(M)
    D = x32.shape[1]
    dk = D // RET_HEADS
    qkvg = matmul(x16, [(w["in"], 0)], n=6 * D, tm=tm, tn=TN_WIDE, name="ret_in")
    if L % RET_CHUNK == 0:
        C, c_real, Lp = RET_CHUNK, RET_CHUNK, L
        cos_t, sin_t = _rotation_tables(pos0 + jnp.arange(L, dtype=jnp.int32), dk)
        y, s_new = retention(qkvg, s0, w["gn_g"], cos_t, sin_t, B=B, L=L, H=RET_HEADS, C=C, c_real=c_real)
    else:
        C, c_real, Lp = SAMPLE_ROWS, L, SAMPLE_ROWS
        assert L == 1
        cos_t, sin_t = _rotation_tables(pos0 + jnp.arange(Lp, dtype=jnp.int32), dk)
        padded = jnp.pad(qkvg[:B].reshape(B, L, 6 * D), ((0, 0), (0, Lp - L), (0, 0))).reshape(B * Lp, 6 * D)
        y, s_new = retention(padded, s0, w["gn_g"], cos_t, sin_t, B=B, L=Lp, H=RET_HEADS, C=C, c_real=c_real)
        y = jnp.pad(y.reshape(B, Lp, 2 * D)[:, 0], ((0, M - B), (0, 0)))
    pre1 = matmul(y, [(w["o"], 0)], n=D, tm=min(tm, 512), tn=512, extras=(x32,),
                  epi=lambda acc, xt: ALPHA * xt + acc, name="ret_o")
    return pre1, s_new


def _rwkv_layer(x32, x_prev, w, s0, *, M, B, L):
    tm, tr = _tiles(M)
    D = x32.shape[1]
    xr, xk, xv, xw, xa, xg = shift_mix(x32, x_prev, w["mu"], tm=min(tr, 128))
    rkv, lead = w["rkv"]
    r = matmul(xr, [((rkv, lead + (0,)), 0)], n=D, tm=tm, tn=TN_WIDE, name="rwkv_r")
    k = matmul(xk, [((rkv, lead + (1,)), 0)], n=D, tm=tm, tn=TN_WIDE, name="rwkv_k")
    v = matmul(xv, [((rkv, lead + (2,)), 0)], n=D, tm=tm, tn=TN_WIDE, name="rwkv_v")
    lora = _mat_shape(w["w1"])[1]
    wmid = matmul(xw, [(w["w1"], 0)], n=lora, tm=tm, tn=lora, epi=jnp.tanh, out_dtypes=(BF16,), name="rwkv_w1")

    def log_decay(acc, w0):
        z = -(w0 + acc)
        softplus = jnp.maximum(z, 0.0) + jnp.log1p(jnp.exp(-jnp.abs(z)))
        return -jnp.exp(-softplus - 0.5)

    lw = matmul(wmid, [(w["w2"], 0)], n=D, tm=tm, tn=512, extras=((w["w0"], 0),), epi=log_decay, name="rwkv_w2")
    amid = matmul(xa, [(w["a1"], 0)], n=lora, tm=tm, tn=lora, out_dtypes=(BF16,), name="rwkv_a1")
    a = matmul(amid, [(w["a2"], 0)], n=D, tm=tm, tn=512, extras=((w["a0"], 0),),
               epi=lambda acc, a0: jax.nn.sigmoid(a0 + acc), name="rwkv_a2")
    gl = _mat_shape(w["g1"])[1]
    gmid = matmul(xg, [(w["g1"], 0)], n=gl, tm=tm, tn=gl, epi=jax.nn.sigmoid, out_dtypes=(BF16,), name="rwkv_g1")
    g = matmul(gmid, [(w["g2"], 0)], n=D, tm=tm, tn=512, name="rwkv_g2")
    params = (w["k_k"], w["k_a"], w["r_k"], w["gn_g"], w["gn_b"])
    if L % RWKV_CHUNK == 0:
        y, s_new = rwkv_recurrence(r, k, v, lw, a, g, *params, s0, B=B, L=L, C=RWKV_CHUNK)
    else:
        assert L == 1
        Lp = SAMPLE_ROWS
        pad = lambda t: jnp.pad(t[:B].reshape(B, L, D), ((0, 0), (0, Lp - L), (0, 0))).reshape(B * Lp, D)
        y, s_new = rwkv_recurrence(*[pad(t) for t in (r, k, v, lw, a, g)], *params, s0, B=B, L=Lp, C=Lp)
        y = jnp.pad(y.reshape(B, Lp, D)[:, 0], ((0, M - B), (0, 0)))
    pre1 = matmul(y, [(w["o"], 0)], n=D, tm=tm, tn=512, extras=(x32,),
                  epi=lambda acc, xt: ALPHA * xt + acc, name="rwkv_o")
    return pre1, s_new


def kernel(x_prompt, x_sample, p_prompt, p_sample, cache_moba_k, cache_moba_v, page_table, state_ret, state_wkv, state_shift, ln1_g, ln1_b, ln2_g, ln2_b, w_ffn_in, w_ffn_out, w_ple_gate, w_ple_proj, rel_bias, a_w_qkv, a_w_o, r_w_in, r_gn_g, r_w_o, c_mu, c_w_rkv, c_w0, c_w1, c_w2, c_a0, c_a1, c_a2, c_g1, c_g2, c_k_k, c_k_a, c_r_k, c_gn_g, c_gn_b, c_w_o):
    Bp, Sp, D = x_prompt.shape
    Bs, Ts, _ = x_sample.shape
    assert Ts == 1 and Bs <= SAMPLE_ROWS
    Mp, Ms = Bp * Sp, SAMPLE_ROWS
    past = page_table.shape[1] * PAGE_SIZE
    bf = lambda t: t.astype(BF16)
    pad_rows = lambda t: jnp.pad(t, ((0, Ms - Bs), (0, 0)))

    xp32 = x_prompt.reshape(Mp, D)
    xs32 = pad_rows(x_sample.reshape(Bs, D))
    xp16, xs16 = bf(xp32), bf(xs32)
    k_pool, v_pool = cache_moba_k, cache_moba_v
    gpad = W_GATE_LORA_PAD - c_g1.shape[-1]
    w16 = dict(ffn_in=bf(w_ffn_in), ffn_out=bf(w_ffn_out), ple_gate=bf(w_ple_gate), ple_proj=bf(w_ple_proj),
               qkv=bf(a_w_qkv), a_o=bf(a_w_o), r_in=bf(r_w_in), r_o=bf(r_w_o), rkv=bf(c_w_rkv), w1=bf(c_w1),
               w2=bf(c_w2), a1=bf(c_a1), a2=bf(c_a2), g1=bf(jnp.pad(c_g1, ((0, 0), (0, 0), (0, gpad)))),
               g2=bf(jnp.pad(c_g2, ((0, 0), (0, gpad), (0, 0)))), c_o=bf(c_w_o))

    kp_l, vp_l, ks_l, vs_l, rp_l, rs_l, wp_l, ws_l, hp_l, hs_l = ([] for _ in range(10))
    for i in range(DEPTH):
        kind, j = i % 3, i // 3
        tail = dict(ln1_g=ln1_g[i], ln1_b=ln1_b[i], ln2_g=ln2_g[i], ln2_b=ln2_b[i],
                    ffn_in=(w16["ffn_in"], (i,)), ffn_out=(w16["ffn_out"], (i,)),
                    ple_gate=(w16["ple_gate"], (i,)), ple_proj=(w16["ple_proj"], (i,)))
        if kind == 0:
            w = dict(qkv=(w16["qkv"], (j,)), o=(w16["a_o"], (j,)))
            pre_p, kp, vp = _moba_layer(xp32, xp16, w, rel_bias, M=Mp, prompt=True, B=Bp, S=Sp)
            pre_s, ks, vs = _moba_layer(xs32, xs16, w, rel_bias, M=Ms, prompt=False, B=Bs, k_pool=k_pool,
                                        v_pool=v_pool, layer=j, page_table=page_table)
            kp_l.append(kp.reshape(Bp, Sp, A_HEADS, D // A_HEADS))
            vp_l.append(vp.reshape(Bp, Sp, A_HEADS, D // A_HEADS))
            ks_l.append(ks[:Bs].reshape(Bs, Ts, A_HEADS, D // A_HEADS))
            vs_l.append(vs[:Bs].reshape(Bs, Ts, A_HEADS, D // A_HEADS))
        elif kind == 1:
            w = dict(o=(w16["r_o"], (j,)), gn_g=r_gn_g[j])
            w["in"] = (w16["r_in"], (j,))
            s0 = jnp.zeros((Bp,) + state_ret.shape[2:], state_ret.dtype)
            pre_p, sp = _ret_layer(xp32, xp16, w, s0, 0, M=Mp, B=Bp, L=Sp)
            pre_s, ss = _ret_layer(xs32, xs16, w, state_ret[j], past, M=Ms, B=Bs, L=Ts)
            rp_l.append(sp)
            rs_l.append(ss)
        else:
            w = dict(mu=c_mu[j], w0=c_w0[j].reshape(1, D), a0=c_a0[j].reshape(1, D),
                     k_k=c_k_k[j], k_a=c_k_a[j], r_k=c_r_k[j], gn_g=c_gn_g[j], gn_b=c_gn_b[j], o=(w16["c_o"], (j,)))
            w.update({name: (w16[name], (j,)) for name in ("rkv", "w1", "w2", "a1", "a2", "g1", "g2")})
            xp3 = xp32.reshape(Bp, Sp, D)
            xprev_p = jnp.concatenate([jnp.zeros((Bp, 1, D), F32), xp3[:, :-1]], axis=1).reshape(Mp, D)
            xprev_s = pad_rows(state_shift[j])
            wk0 = jnp.zeros((Bp,) + state_wkv.shape[2:], state_wkv.dtype)
            pre_p, wkp = _rwkv_layer(xp32, xprev_p, w, wk0, M=Mp, B=Bp, L=Sp)
            pre_s, wks = _rwkv_layer(xs32, xprev_s, w, state_wkv[j], M=Ms, B=Bs, L=Ts)
            wp_l.append(wkp)
            ws_l.append(wks)
            hp_l.append(xp3[:, -1])
            hs_l.append(xs32[:Bs])
        xp32, xp16 = _layer_tail(pre_p, bf(p_prompt[i].reshape(Mp, -1)), tail, M=Mp)
        xs32, xs16 = _layer_tail(pre_s, bf(pad_rows(p_sample[i].reshape(Bs, -1))), tail, M=Ms)
    return (xp32.reshape(Bp, Sp, D), xs32[:Bs].reshape(Bs, Ts, D),
            jnp.stack(kp_l), jnp.stack(vp_l), jnp.stack(ks_l), jnp.stack(vs_l),
            jnp.stack(rp_l), jnp.stack(rs_l), jnp.stack(wp_l), jnp.stack(ws_l),
            jnp.stack(hp_l), jnp.stack(hs_l))
```

```python
import functools
import math

import numpy as np
import jax
import jax.numpy as jnp
from jax import lax
from jax.experimental import pallas as pl
from jax.experimental.pallas import tpu as pltpu

F32 = jnp.float32
BF16 = jnp.bfloat16

DEPTH = 4
LN_EPS = 1e-5
ALPHA = (2 * DEPTH) ** 0.25
A_HEADS = 32
MOBA_BLOCK = 256
MOBA_TOPK = 3
PAGE_SIZE = 128
REL_BUCKETS = 32
REL_MAX_DIST = 128
RET_HEADS = 16
RET_CHUNK = 128
RET_GN_EPS = 1e-5
W_HEAD = 64
W_GN_EPS = 64e-5
W_GATE_LORA_PAD = 512
NEG_INF = -1e30
LOG2E = math.log2(math.e)

V7X_VMEM_LIMIT_BYTES = 56 * 1024 * 1024
LANES = 128

MOBA_HEADS_PER_STEP = 4
RET_HEADS_PER_STEP = 8
RWKV_CHUNK = 64
RWKV_HEADS_PER_STEP = 32
SAMPLE_ROWS = 16


def _t5_bucket_thresholds():
    n = np.arange(0, 4 * REL_MAX_DIST)
    max_exact = REL_BUCKETS // 2
    nf = np.maximum(n, 1).astype(np.float32)
    large = max_exact + (np.log(nf / np.float32(max_exact)) / np.float32(math.log(REL_MAX_DIST / max_exact))
                         * np.float32(REL_BUCKETS - max_exact)).astype(np.int32)
    bucket = np.where(n < max_exact, n, np.minimum(large, REL_BUCKETS - 1))
    assert sorted(set(bucket.tolist())) == list(range(REL_BUCKETS))
    return tuple(int(np.argmax(bucket >= b)) for b in range(REL_BUCKETS))


T5_THRESHOLDS = _t5_bucket_thresholds()


def _params(*sem):
    return pltpu.CompilerParams(dimension_semantics=sem, vmem_limit_bytes=V7X_VMEM_LIMIT_BYTES)


def _bias_from_distance(dist, table_ref, h):
    t = jnp.full(dist.shape, table_ref[0, h], F32)
    for b in range(1, REL_BUCKETS):
        t = jnp.where(dist >= T5_THRESHOLDS[b], table_ref[b, h], t)
    return t


def _mm_kernel(epi, n_w, n_ex, x_ref, *refs):
    w_refs = refs[:n_w]
    ex_refs = refs[n_w:n_w + n_ex]
    o_refs = refs[n_w + n_ex:]
    x = x_ref[...]
    accs = [jnp.dot(x, w[...], preferred_element_type=F32) for w in w_refs]
    outs = epi(*accs, *[e[...] for e in ex_refs])
    if not isinstance(outs, tuple):
        outs = (outs,)
    for o_ref, o in zip(o_refs, outs):
        o_ref[...] = o.astype(o_ref.dtype)


def matmul(x, ws, *, n, tm, tn, epi=None, extras=(), out_dtypes=(F32,), x_buffers=2, name="mm"):
    M, K = x.shape
    assert M % tm == 0 and n % tn == 0, (M, tm, n, tn)
    if epi is None:
        epi = lambda a: a
    in_specs = [pl.BlockSpec((tm, K), lambda i, j: (i, 0), pipeline_mode=pl.Buffered(x_buffers))]
    args = [x]
    for w, col0 in ws:
        w, lead = w if isinstance(w, tuple) else (w, ())
        assert w.shape[-2] == K and col0 % tn == 0 and w.ndim == len(lead) + 2
        in_specs.append(pl.BlockSpec((None,) * len(lead) + (K, tn),
                                     lambda i, j, cb=col0 // tn, lead=lead: lead + (0, j + cb)))
        args.append(w)
    for e in extras:
        if isinstance(e, tuple):
            row, col0 = e
            assert row.shape[0] == 1 and col0 % tn == 0
            in_specs.append(pl.BlockSpec((1, tn), lambda i, j, cb=col0 // tn: (0, j + cb)))
            args.append(row)
        else:
            assert e.shape == (M, n)
            in_specs.append(pl.BlockSpec((tm, tn), lambda i, j: (i, j)))
            args.append(e)
    out_shape = tuple(jax.ShapeDtypeStruct((M, n), dt) for dt in out_dtypes)
    out_specs = tuple(pl.BlockSpec((tm, tn), lambda i, j: (i, j)) for _ in out_dtypes)
    res = pl.pallas_call(
        functools.partial(_mm_kernel, epi, len(ws), len(extras)),
        grid=(M // tm, n // tn),
        in_specs=in_specs,
        out_specs=out_specs,
        out_shape=out_shape,
        compiler_params=_params("arbitrary", "arbitrary"),
        name=name,
    )(*args)
    return res if len(out_dtypes) > 1 else res[0]


def _ln_kernel(x_ref, g_ref, b_ref, o32_ref, o16_ref):
    x = x_ref[...]
    mu = jnp.mean(x, axis=-1, keepdims=True)
    xc = x - mu
    var = jnp.mean(xc * xc, axis=-1, keepdims=True)
    y = xc * lax.rsqrt(var + LN_EPS) * g_ref[...] + b_ref[...]
    o32_ref[...] = y
    o16_ref[...] = y.astype(BF16)


def layer_norm(x, g, b, *, tm):
    M, D = x.shape
    row = pl.BlockSpec((1, D), lambda i: (0, 0))
    tile = pl.BlockSpec((tm, D), lambda i: (i, 0))
    return pl.pallas_call(
        _ln_kernel,
        grid=(M // tm,),
        in_specs=[tile, row, row],
        out_specs=(tile, tile),
        out_shape=(jax.ShapeDtypeStruct((M, D), F32), jax.ShapeDtypeStruct((M, D), BF16)),
        compiler_params=_params("arbitrary"),
        name="layer_norm",
    )(x, g.reshape(1, D), b.reshape(1, D))


def _moba_prompt_kernel(nb, G, tab_ref, q_ref, k_ref, v_ref, o_ref, bown, bprev, kmean, vt, m_s, l_s, acc_s):
    hg = pl.program_id(0)
    b = pl.program_id(1)
    qb = pl.program_id(2)
    blk = MOBA_BLOCK
    n_cand = nb - 1
    k_sel_n = min(MOBA_TOPK, n_cand)
    hd = q_ref.shape[1] // G
    score_scale = hd ** -0.5 * LOG2E
    h0 = hg * G
    key = lax.broadcasted_iota(jnp.int32, (blk, blk), 0)
    qry = lax.broadcasted_iota(jnp.int32, (blk, blk), 1)
    heads = lambda x: jnp.stack([x[:, g * hd:(g + 1) * hd] for g in range(G)])

    @pl.when((b == 0) & (qb == 0))
    def _():
        for g in range(G):
            bown[g] = _bias_from_distance(jnp.maximum(qry - key, 0), tab_ref, h0 + g) * LOG2E
            bprev[g] = _bias_from_distance(qry - key + blk, tab_ref, h0 + g) * LOG2E

    @pl.when(qb == 0)
    def _():
        kmean[...] = jnp.zeros(kmean.shape, F32)
        for n in range(n_cand):
            mean_n = jnp.mean(k_ref[n * blk:(n + 1) * blk, :], axis=0, keepdims=True)
            for g in range(G):
                kmean[g, n:n + 1, :] = mean_n[:, g * hd:(g + 1) * hd]
        for n in range(nb):
            for g in range(G):
                vt[n, g] = v_ref[n * blk:(n + 1) * blk, g * hd:(g + 1) * hd].T.astype(BF16)

    q = heads(q_ref[...])
    q16 = q.astype(BF16)
    dn_kq = (((2,), (2,)), ((0,), (0,)))
    dn_vp = (((2,), (1,)), ((0,), (0,)))

    start = pl.multiple_of(qb * blk, blk)
    k_own = heads(k_ref[pl.ds(start, blk), :]).astype(BF16)
    s = lax.dot_general(k_own, q16, dn_kq, preferred_element_type=F32) * score_scale + bown[...]
    s = jnp.where(qry >= key, s, NEG_INF)
    m0 = jnp.max(s, axis=1, keepdims=True)
    p = jnp.exp2(s - m0)
    m_s[...] = m0
    l_s[...] = jnp.sum(p, axis=1, keepdims=True)
    acc_s[...] = lax.dot_general(vt[qb], p.astype(BF16), dn_vp, preferred_element_type=F32)

    if n_cand > 0:
        ncols = kmean.shape[1]
        gate = lax.dot_general(kmean[...], q, dn_kq, precision=lax.Precision.HIGHEST,
                               preferred_element_type=F32)
        cand = lax.broadcasted_iota(jnp.int32, (G, ncols, blk), 1)
        gate = jnp.where(cand < qb, gate, NEG_INF)
        rank = jnp.zeros((G, ncols, blk), F32)
        for m in range(n_cand):
            gm = gate[:, m:m + 1, :]
            ahead = (gm > gate) | ((gm == gate) & (cand > m))
            rank = rank + jnp.where(ahead, 1.0, 0.0)
        sel = jnp.where((rank < k_sel_n) & (cand < qb), 1.0, 0.0)
        gid = lax.broadcasted_iota(jnp.int32, (G, 1, 1), 0)
        far_bias = jnp.full((G, 1, 1), tab_ref[REL_BUCKETS - 1, h0], F32)
        for g in range(1, G):
            far_bias = jnp.where(gid == g, tab_ref[REL_BUCKETS - 1, h0 + g], far_bias)
        far_bias = far_bias * LOG2E

        def attend(n, just_before):
            kn = heads(k_ref[n * blk:(n + 1) * blk, :]).astype(BF16)
            bias = bprev[...] if just_before else far_bias
            sn = lax.dot_general(kn, q16, dn_kq, preferred_element_type=F32) * score_scale + bias
            sn = jnp.where(sel[:, n:n + 1, :] > 0.5, sn, NEG_INF)
            m_old = m_s[...]
            m_new = jnp.maximum(m_old, jnp.max(sn, axis=1, keepdims=True))
            scale_old = jnp.exp2(m_old - m_new)
            pn = jnp.exp2(sn - m_new)
            l_s[...] = scale_old * l_s[...] + jnp.sum(pn, axis=1, keepdims=True)
            acc_s[...] = scale_old * acc_s[...] + lax.dot_general(vt[n], pn.astype(BF16), dn_vp,
                                                                  preferred_element_type=F32)
            m_s[...] = m_new

        for n in range(n_cand):
            pl.when(n == qb - 1)(functools.partial(attend, n, True))
            pl.when(n < qb - 1)(functools.partial(attend, n, False))

    out = acc_s[...] / l_s[...]
    for g in range(G):
        o_ref[:, g * hd:(g + 1) * hd] = out[g].T.astype(o_ref.dtype)


def moba_prompt(q, k, v, rel_bias, *, B, S, H):
    M, D = q.shape
    hd = D // H
    blk = MOBA_BLOCK
    G = math.gcd(H, MOBA_HEADS_PER_STEP)
    assert S % blk == 0 and M == B * S and T5_THRESHOLDS[-1] <= blk
    nb = S // blk
    ncols = max(8, -(-(nb - 1) // 8) * 8)
    qspec = pl.BlockSpec((blk, G * hd), lambda hg, b, qb: (b * nb + qb, hg))
    kvspec = pl.BlockSpec((S, G * hd), lambda hg, b, qb: (b, hg))
    return pl.pallas_call(
        functools.partial(_moba_prompt_kernel, nb, G),
        grid=(H // G, B, nb),
        in_specs=[pl.BlockSpec(memory_space=pltpu.SMEM), qspec, kvspec, kvspec],
        out_specs=qspec,
        out_shape=jax.ShapeDtypeStruct((M, D), BF16),
        scratch_shapes=[pltpu.VMEM((G, blk, blk), F32), pltpu.VMEM((G, blk, blk), F32),
                        pltpu.VMEM((G, ncols, hd), F32), pltpu.VMEM((nb, G, hd, blk), BF16),
                        pltpu.VMEM((G, 1, blk), F32), pltpu.VMEM((G, 1, blk), F32),
                        pltpu.VMEM((G, hd, blk), F32)],
        compiler_params=_params("arbitrary", "arbitrary", "arbitrary"),
        name="moba_prompt",
    )(rel_bias, q, k, v)


def _moba_select_kernel(n_blocks, pt_ref, pa_ref, pb_ref, q_ref, sel_ref, sums):
    j = pl.program_id(1)
    sums[j] = jnp.sum(pa_ref[0, 0], axis=0) + jnp.sum(pb_ref[0, 0], axis=0)

    @pl.when(j == n_blocks - 1)
    def _():
        prod = (sums[...] * (1.0 / MOBA_BLOCK)) * q_ref[...]
        gate = jnp.sum(prod, axis=-1, keepdims=True)
        blk_id = lax.broadcasted_iota(jnp.int32, gate.shape, 0)
        rank = jnp.zeros(gate.shape, F32)
        for m in range(n_blocks):
            gm = gate[m:m + 1]
            ahead = (gm > gate) | ((gm == gate) & (blk_id > m))
            rank = rank + jnp.where(ahead, 1.0, 0.0)
        for r in range(sel_ref.shape[1]):
            sel_ref[0, r] = jnp.sum(jnp.where(rank == float(r), blk_id, 0), axis=0)


def _moba_sample_attn_kernel(n_sel_pages, past, layer, n_pages, pt_ref, sel_ref, tab_ref, q_ref, kn_ref, vn_ref,
                             k_hbm, v_hbm, o_ref, kbuf, vbuf, sem):
    b = pl.program_id(0)
    h = pl.program_id(1)
    n_heads = pl.num_programs(1)
    step = b * n_heads + h
    n_steps = pl.num_programs(0) * n_heads
    hd = q_ref.shape[-1]
    a_scale = hd ** -0.5
    pages_per_block = MOBA_BLOCK // PAGE_SIZE

    def page_copies(s, slot):
        bb = lax.div(s, n_heads)
        hh = lax.rem(s, n_heads)
        out = []
        for t in range(n_sel_pages):
            blk = sel_ref[s * MOBA_TOPK + t // pages_per_block]
            page = pt_ref[bb * n_pages + pages_per_block * blk + t % pages_per_block]
            out.append(pltpu.make_async_copy(k_hbm.at[layer, page, :, hh, :], kbuf.at[slot, t], sem.at[0, slot, t]))
            out.append(pltpu.make_async_copy(v_hbm.at[layer, page, :, hh, :], vbuf.at[slot, t], sem.at[1, slot, t]))
        return out

    slot = lax.rem(step, 2)

    @pl.when(step == 0)
    def _():
        for c in page_copies(step, slot):
            c.start()

    @pl.when(step + 1 < n_steps)
    def _():
        for c in page_copies(step + 1, 1 - slot):
            c.start()

    for c in page_copies(step, slot):
        c.wait()

    q = q_ref[0]
    s_own = jnp.sum(q * kn_ref[0], axis=-1, keepdims=True) * a_scale + tab_ref[0, h]
    r = lax.broadcasted_iota(jnp.int32, (PAGE_SIZE, 1), 0)
    scores = []
    for t in range(n_sel_pages):
        blk = sel_ref[step * MOBA_TOPK + t // pages_per_block]
        pos0 = blk * MOBA_BLOCK + (t % pages_per_block) * PAGE_SIZE
        dist = past - (pos0 + r)
        bias = _bias_from_distance(dist, tab_ref, h)
        scores.append(jnp.sum(kbuf[slot, t] * q, axis=-1, keepdims=True) * a_scale + bias)
    m = s_own
    for s in scores:
        m = jnp.maximum(m, jnp.max(s, axis=0, keepdims=True))
    p_own = jnp.exp(s_own - m)
    l = p_own
    acc = p_own * vn_ref[0]
    for t, s in enumerate(scores):
        p = jnp.exp(s - m)
        l = l + jnp.sum(p, axis=0, keepdims=True)
        acc = acc + jnp.sum(p * vbuf[slot, t], axis=0, keepdims=True)
    o_ref[0] = (acc / l).astype(o_ref.dtype)


def moba_sample(q, k_new, v_new, k_pool, v_pool, layer, page_table, rel_bias, *, H):
    rows, D = q.shape
    B, n_pages = page_table.shape
    assert rows == B
    hd = D // H
    past = n_pages * PAGE_SIZE
    ppb = MOBA_BLOCK // PAGE_SIZE
    assert past % MOBA_BLOCK == 0
    n_blocks = past // MOBA_BLOCK
    assert n_blocks >= MOBA_TOPK
    pt = page_table.reshape(-1)
    q3 = q.reshape(rows, 1, D)

    page = lambda half: pl.BlockSpec(
        (1, 1, PAGE_SIZE, H, hd), lambda b, j, pt_ref: (layer, pt_ref[b * n_pages + ppb * j + half], 0, 0, 0))
    sel = pl.pallas_call(
        functools.partial(_moba_select_kernel, n_blocks),
        grid_spec=pltpu.PrefetchScalarGridSpec(
            num_scalar_prefetch=1,
            grid=(B, n_blocks),
            in_specs=[page(0), page(1), pl.BlockSpec((1, H, hd), lambda b, j, pt_ref: (b, 0, 0))],
            out_specs=pl.BlockSpec((1, MOBA_TOPK, H, 1), lambda b, j, pt_ref: (b, 0, 0, 0)),
            scratch_shapes=[pltpu.VMEM((n_blocks, H, hd), F32)]),
        out_shape=jax.ShapeDtypeStruct((B, MOBA_TOPK, H, 1), jnp.int32),
        compiler_params=_params("arbitrary", "arbitrary"),
        name="moba_sample_select",
    )(pt, k_pool, k_pool, q.reshape(B, H, hd))
    sel_flat = jnp.transpose(sel[..., 0], (0, 2, 1)).reshape(-1)

    n_sel_pages = MOBA_TOPK * ppb
    tok = pl.BlockSpec((1, 1, hd), lambda b, h, pt_ref, sel_ref: (b, 0, h))
    hbm = pl.BlockSpec(memory_space=pl.ANY)
    out = pl.pallas_call(
        functools.partial(_moba_sample_attn_kernel, n_sel_pages, past, layer, n_pages),
        grid_spec=pltpu.PrefetchScalarGridSpec(
            num_scalar_prefetch=2,
            grid=(B, H),
            in_specs=[pl.BlockSpec(memory_space=pltpu.SMEM), tok, tok, tok, hbm, hbm],
            out_specs=tok,
            scratch_shapes=[pltpu.VMEM((2, n_sel_pages, PAGE_SIZE, hd), F32),
                            pltpu.VMEM((2, n_sel_pages, PAGE_SIZE, hd), F32),
                            pltpu.SemaphoreType.DMA((2, 2, n_sel_pages))]),
        out_shape=jax.ShapeDtypeStruct((rows, 1, D), BF16),
        compiler_params=_params("arbitrary", "arbitrary"),
        name="moba_sample_attn",
    )(pt, sel_flat, rel_bias, q3, k_new.reshape(rows, 1, D), v_new.reshape(rows, 1, D), k_pool, v_pool)
    return out.reshape(rows, D)


def _retention_kernel(c_real, lg_ref, q_ref, k_ref, v_ref, g_ref, cos_ref, sin_ref, gn_ref, s0_ref,
                      y_ref, s_out_ref, s_acc):
    hg = pl.program_id(1)
    c = pl.program_id(2)
    n_chunks = pl.num_programs(2)
    C, dk = cos_ref.shape
    G = q_ref.shape[1] // dk
    dv = v_ref.shape[1] // G

    @pl.when(c == 0)
    def _():
        s_acc[...] = s0_ref[0]

    gid = lax.broadcasted_iota(jnp.int32, (G, 1, 1), 0)
    lg = jnp.full((G, 1, 1), lg_ref[hg * G], F32)
    for g in range(1, G):
        lg = jnp.where(gid == g, lg_ref[hg * G + g], lg)
    cos = cos_ref[...]
    sin = sin_ref[...]
    even = (lax.broadcasted_iota(jnp.int32, (C, LANES), 1) % 2) == 0
    heads = lambda x, w: jnp.stack([x[:, g * w:(g + 1) * w] for g in range(G)])

    def rotate(x):
        tiles = []
        for t in range(dk // LANES):
            xt = x[:, t * LANES:(t + 1) * LANES]
            tiles.append(jnp.where(even, pltpu.roll(xt, LANES - 1, 1), pltpu.roll(xt, 1, 1)))
        return x * cos + jnp.concatenate(tiles, axis=1) * sin

    q = jnp.stack([rotate(q_ref[:, g * dk:(g + 1) * dk]) for g in range(G)])
    k = jnp.stack([rotate(k_ref[:, g * dk:(g + 1) * dk]) for g in range(G)]) * (dk ** -0.5)
    v16 = heads(v_ref[...], dv).astype(BF16)
    idx = lax.broadcasted_iota(jnp.int32, (1, C, 1), 1).astype(F32)
    q_dec = jnp.exp((idx + 1.0) * lg)
    k_dec = jnp.exp((c_real - 1.0 - idx) * lg)
    ii = lax.broadcasted_iota(jnp.int32, (1, C, C), 1)
    jj = lax.broadcasted_iota(jnp.int32, (1, C, C), 2)
    diff = (ii - jj).astype(F32)
    dmask = jnp.where(diff >= 0, jnp.exp(jnp.maximum(diff, 0.0) * lg), 0.0)
    dg = lambda a, b, dn: lax.dot_general(a.astype(BF16), b.astype(BF16), dn, preferred_element_type=F32)
    qk = dg(q, k, _DN_NT) * dmask
    s_prev = s_acc[...]
    o = dg(qk, v16, _DN_NN) + dg(q * q_dec, s_prev, _DN_NN)
    kv = dg(k * k_dec, v16, _DN_TN)
    c_dec = jnp.exp(c_real * lg)
    s_new = s_prev * c_dec + kv
    s_acc[...] = s_new

    mu = jnp.mean(o, axis=-1, keepdims=True)
    oc = o - mu
    var = jnp.mean(oc * oc, axis=-1, keepdims=True)
    gate = heads(g_ref[...], dv)
    y = oc * lax.rsqrt(var + RET_GN_EPS) * heads(gn_ref[...], dv) * (gate * jax.nn.sigmoid(gate))
    for g in range(G):
        y_ref[:, g * dv:(g + 1) * dv] = y[g].astype(y_ref.dtype)

    @pl.when(c == n_chunks - 1)
    def _():
        s_out_ref[0] = s_new


def retention(qkvg, s0, gn_g, cos_t, sin_t, *, B, L, H, C, c_real):
    M, N6 = qkvg.shape
    D = N6 // 6
    dk, dv = D // H, 2 * D // H
    assert L % C == 0 and M == B * L
    nC = L // C
    G = math.gcd(H, RET_HEADS_PER_STEP)
    nG = H // G
    log_g = jnp.log1p(-jnp.exp2(-5.0 - jnp.arange(H, dtype=F32)))
    tokspec = lambda width, sec: pl.BlockSpec((C, G * width), lambda b, hg, c: (b * nC + c, sec * nG + hg))
    tabspec = pl.BlockSpec((C, dk), lambda b, hg, c: (c, 0))
    sspec = pl.BlockSpec((1, G, dk, dv), lambda b, hg, c: (b, hg, 0, 0))
    return pl.pallas_call(
        functools.partial(_retention_kernel, float(c_real)),
        grid=(B, nG, nC),
        in_specs=[pl.BlockSpec(memory_space=pltpu.SMEM),
                  tokspec(dk, 0), tokspec(dk, 1), tokspec(dv, 1), tokspec(dv, 2),
                  tabspec, tabspec, pl.BlockSpec((1, G * dv), lambda b, hg, c: (0, hg)), sspec],
        out_specs=(pl.BlockSpec((C, G * dv), lambda b, hg, c: (b * nC + c, hg)), sspec),
        out_shape=(jax.ShapeDtypeStruct((M, 2 * D), BF16), jax.ShapeDtypeStruct(s0.shape, F32)),
        scratch_shapes=[pltpu.VMEM((G, dk, dv), F32)],
        compiler_params=_params("arbitrary", "arbitrary", "arbitrary"),
        name="retention",
    )(log_g, qkvg, qkvg, qkvg, qkvg, cos_t, sin_t, gn_g.reshape(1, 2 * D), s0)


def _rotation_tables(pos, dk):
    inv = 1.0 / (10000.0 ** jnp.linspace(0.0, 1.0, dk // 2, dtype=F32))
    ang = pos.astype(F32)[:, None] * inv
    cos = jnp.repeat(jnp.cos(ang), 2, axis=1)
    sin = jnp.stack([-jnp.sin(ang), jnp.sin(ang)], axis=-1).reshape(pos.shape[0], dk)
    return cos, sin


def _shift_mix_kernel(x_ref, xp_ref, mu_ref, *o_refs):
    x = x_ref[...]
    xx = xp_ref[...] - x
    for i, o_ref in enumerate(o_refs):
        o_ref[...] = (x + xx * mu_ref[i:i + 1, :]).astype(o_ref.dtype)


def shift_mix(x, x_prev, mu, *, tm):
    M, D = x.shape
    n = mu.shape[0]
    tile = pl.BlockSpec((tm, D), lambda i: (i, 0))
    return pl.pallas_call(
        _shift_mix_kernel,
        grid=(M // tm,),
        in_specs=[tile, tile, pl.BlockSpec((n, D), lambda i: (0, 0))],
        out_specs=tuple(tile for _ in range(n)),
        out_shape=tuple(jax.ShapeDtypeStruct((M, D), BF16) for _ in range(n)),
        compiler_params=_params("arbitrary"),
        name="rwkv_shift_mix",
    )(x, x_prev, mu)


def _split3(x):
    hi = x.astype(BF16)
    r1 = x - hi.astype(F32)
    mid = r1.astype(BF16)
    lo = (r1 - mid.astype(F32)).astype(BF16)
    return hi, mid, lo


def _dot16(a, b, dn):
    return lax.dot_general(a.astype(BF16), b.astype(BF16), dimension_numbers=dn, preferred_element_type=F32)


_DN_NN = (((2,), (1,)), ((0,), (0,)))
_DN_NT = (((2,), (2,)), ((0,), (0,)))
_DN_TN = (((1,), (1,)), ((0,), (0,)))


def _rwkv_kernel(r_ref, k_ref, v_ref, lw_ref, a_ref, g_ref, kk_ref, ka_ref, rk_ref, gng_ref, gnb_ref, s0_ref,
                 y_ref, s_out_ref, s_acc):
    c = pl.program_id(2)
    n_chunks = pl.num_programs(2)
    C = r_ref.shape[0]
    N = W_HEAD
    G = r_ref.shape[1] // N

    @pl.when(c == 0)
    def _():
        s_acc[...] = s0_ref[0]

    ti = lax.broadcasted_iota(jnp.int32, (C, C), 0)
    tj = lax.broadcasted_iota(jnp.int32, (C, C), 1)
    strict = ti > tj
    incl = ti >= tj
    tri16 = jnp.broadcast_to(jnp.where(incl, 1.0, 0.0).astype(BF16), (G, C, C))
    n_double = max(1, (C - 1).bit_length())
    ti2 = lax.broadcasted_iota(jnp.int32, (C, 2 * C), 0)
    tj2 = lax.broadcasted_iota(jnp.int32, (C, 2 * C), 1)
    k_cols = tj2 >= C
    tj2 = jnp.where(k_cols, tj2 - C, tj2)
    strict_k = (ti2 > tj2) & k_cols
    incl2 = ti2 >= tj2

    heads = lambda ref: jnp.stack([ref[:, hh * N:(hh + 1) * N] for hh in range(G)])
    r, k, v, lw, a = heads(r_ref), heads(k_ref), heads(v_ref), heads(lw_ref), heads(a_ref)
    kk = k * heads(kk_ref)
    kk = kk / jnp.maximum(jnp.sqrt(jnp.sum(kk * kk, axis=-1, keepdims=True)), 1e-12)
    kh = k * (1.0 + (a - 1.0) * heads(ka_ref))
    av = -kk
    bv = kk * a
    dg = functools.partial(lax.dot_general, dimension_numbers=_DN_NN, preferred_element_type=F32)
    hi, mid, lo = _split3(lw)
    cum = dg(tri16, hi) + dg(tri16, mid) + dg(tri16, lo)
    e_in = jnp.exp(cum)
    e_ex = jnp.exp(cum - lw)
    e_neg = jnp.exp(-cum)
    at = av * e_ex
    rt = r * e_in
    bt = bv * e_neg
    kt = kh * e_neg
    s0 = s_acc[...]
    lh = jnp.concatenate([at, rt], axis=1)
    rh = jnp.concatenate([bt, kt], axis=1)
    amat = _dot16(lh, rh, _DN_NT)
    z = _dot16(lh, s0, _DN_NT)
    zero_v = jnp.zeros_like(v)
    a_ab = jnp.where(strict, amat[:, :C, :C], 0.0)
    a_ak = jnp.where(strict_k, amat[:, :C, :], 0.0)
    u = z[:, :C] + _dot16(a_ak, jnp.concatenate([zero_v, v], axis=1), _DN_NN)
    pw = a_ab
    for it in range(n_double):
        u = u + _dot16(pw, u, _DN_NN)
        if it + 1 < n_double:
            pw = _dot16(pw, pw, _DN_NN)
    uv = jnp.concatenate([u, v], axis=1)
    y = z[:, C:] + _dot16(jnp.where(incl2, amat[:, C:, :], 0.0), uv, _DN_NN)
    wc = e_in[:, C - 1:C, :]
    s_acc[...] = s0 * wc + _dot16(uv, rh * wc, _DN_TN)

    mu = jnp.mean(y, axis=-1, keepdims=True)
    yc = y - mu
    var = jnp.mean(yc * yc, axis=-1, keepdims=True)
    yn = yc * lax.rsqrt(var + W_GN_EPS)
    bonus = jnp.sum(r * kh * heads(rk_ref), axis=-1, keepdims=True) * v
    out = (yn * heads(gng_ref) + heads(gnb_ref) + bonus) * heads(g_ref)
    for hh in range(G):
        y_ref[:, hh * N:(hh + 1) * N] = out[hh].astype(y_ref.dtype)

    @pl.when(c == n_chunks - 1)
    def _():
        s_out_ref[0] = s_acc[...]


def rwkv_recurrence(r, k, v, lw, a, g, k_k, k_a, r_k, gn_g, gn_b, s0, *, B, L, C):
    M, D = r.shape
    N = W_HEAD
    H = D // N
    G = math.gcd(H, RWKV_HEADS_PER_STEP)
    assert L % C == 0 and M == B * L
    nC = L // C
    tok = pl.BlockSpec((C, G * N), lambda b, hg, c: (b * nC + c, hg))
    row = pl.BlockSpec((1, G * N), lambda b, hg, c: (0, hg))
    sspec = pl.BlockSpec((1, G, N, N), lambda b, hg, c: (b, hg, 0, 0))
    rows = [p.reshape(1, D) for p in (k_k, k_a, r_k, gn_g, gn_b)]
    return pl.pallas_call(
        _rwkv_kernel,
        grid=(B, H // G, nC),
        in_specs=[tok] * 6 + [row] * 5 + [sspec],
        out_specs=(tok, sspec),
        out_shape=(jax.ShapeDtypeStruct((M, D), BF16), jax.ShapeDtypeStruct(s0.shape, F32)),
        scratch_shapes=[pltpu.VMEM((G, N, N), F32)],
        compiler_params=_params("arbitrary", "arbitrary", "arbitrary"),
        name="rwkv_recurrence",
    )(r, k, v, lw, a, g, *rows, s0)


def _tiles(M):
    if M >= 1024:
        return 1024, 256
    return M, M


TN_WIDE = 1024
TN_FFN = 256


def _mat_shape(w):
    return (w[0] if isinstance(w, tuple) else w).shape[-2:]


def _layer_tail(pre1, p16, w, *, M):
    tm, tr = _tiles(M)
    F, D = _mat_shape(w["ffn_out"])
    h32, h16 = layer_norm(pre1, w["ln1_g"], w["ln1_b"], tm=tr)
    act = matmul(h16, [(w["ffn_in"], 0), (w["ffn_in"], F)], n=F, tm=min(M, 2 * tm), tn=TN_FFN,
                 epi=lambda gt, up: gt * jax.nn.sigmoid(gt) * up, out_dtypes=(BF16,), name="ffn_in")
    pp = matmul(p16, [(w["ple_proj"], 0)], n=D, tm=tm, tn=TN_WIDE, name="ple_proj")
    ple = matmul(h16, [(w["ple_gate"], 0)], n=D, tm=tm, tn=512, extras=(pp,),
                 epi=lambda acc, ppt: jax.nn.sigmoid(acc) * ppt, name="ple_gate")
    pre2 = matmul(act, [(w["ffn_out"], 0)], n=D, tm=min(tm, 512), tn=512, extras=(ple, h32),
                  epi=lambda acc, plet, ht: ALPHA * ht + acc + plet, name="ffn_out")
    return layer_norm(pre2, w["ln2_g"], w["ln2_b"], tm=tr)


def _moba_layer(x32, x16, w, rel_bias, *, M, prompt, B, S=None, k_pool=None, v_pool=None, layer=None,
                page_table=None):
    tm, _ = _tiles(M)
    D = x32.shape[1]
    q, k, v = [matmul(x16, [(w["qkv"], i * D)], n=D, tm=tm, tn=TN_WIDE, name="moba_qkv") for i in range(3)]
    if prompt:
        att = moba_prompt(q, k, v, rel_bias, B=B, S=S, H=A_HEADS)
    else:
        att = moba_sample(q[:B], k[:B], v[:B], k_pool, v_pool, layer, page_table, rel_bias, H=A_HEADS)
        att = jnp.pad(att, ((0, M - B), (0, 0)))
    pre1 = matmul(att, [(w["o"], 0)], n=D, tm=tm, tn=512, extras=(x32,),
                  epi=lambda acc, xt: ALPHA * xt + acc, name="moba_o")
    return pre1, k, v


def _ret_layer(x32, x16, w, s0, pos0, *, M, B, L):
    tm, _ = _tiles(M)
    D = x32.shape[1]
    dk = D // RET_HEADS
    qkvg = matmul(x16, [(w["in"], 0)], n=6 * D, tm=tm, tn=TN_WIDE, name="ret_in")
    if L % RET_CHUNK == 0:
        C, c_real, Lp = RET_CHUNK, RET_CHUNK, L
        cos_t, sin_t = _rotation_tables(pos0 + jnp.arange(L, dtype=jnp.int32), dk)
        y, s_new = retention(qkvg, s0, w["gn_g"], cos_t, sin_t, B=B, L=L, H=RET_HEADS, C=C, c_real=c_real)
    else:
        C, c_real, Lp = SAMPLE_ROWS, L, SAMPLE_ROWS
        assert L == 1
        cos_t, sin_t = _rotation_tables(pos0 + jnp.arange(Lp, dtype=jnp.int32), dk)
        padded = jnp.pad(qkvg[:B].reshape(B, L, 6 * D), ((0, 0), (0, Lp - L), (0, 0))).reshape(B * Lp, 6 * D)
        y, s_new = retention(padded, s0, w["gn_g"], cos_t, sin_t, B=B, L=Lp, H=RET_HEADS, C=C, c_real=c_real)
        y = jnp.pad(y.reshape(B, Lp, 2 * D)[:, 0], ((0, M - B), (0, 0)))
    pre1 = matmul(y, [(w["o"], 0)], n=D, tm=min(tm, 512), tn=512, extras=(x32,),
                  epi=lambda acc, xt: ALPHA * xt + acc, name="ret_o")
    return pre1, s_new


def _rwkv_layer(x32, x_prev, w, s0, *, M, B, L):
    tm, tr = _tiles(M)
    D = x32.shape[1]
    xr, xk, xv, xw, xa, xg = shift_mix(x32, x_prev, w["mu"], tm=min(tr, 128))
    rkv, lead = w["rkv"]
    r = matmul(xr, [((rkv, lead + (0,)), 0)], n=D, tm=tm, tn=TN_WIDE, name="rwkv_r")
    k = matmul(xk, [((rkv, lead + (1,)), 0)], n=D, tm=tm, tn=TN_WIDE, name="rwkv_k")
    v = matmul(xv, [((rkv, lead + (2,)), 0)], n=D, tm=tm, tn=TN_WIDE, name="rwkv_v")
    lora = _mat_shape(w["w1"])[1]
    wmid = matmul(xw, [(w["w1"], 0)], n=lora, tm=tm, tn=lora, epi=jnp.tanh, out_dtypes=(BF16,), name="rwkv_w1")

    def log_decay(acc, w0):
        z = -(w0 + acc)
        softplus = jnp.maximum(z, 0.0) + jnp.log1p(jnp.exp(-jnp.abs(z)))
        return -jnp.exp(-softplus - 0.5)

    lw = matmul(wmid, [(w["w2"], 0)], n=D, tm=tm, tn=512, extras=((w["w0"], 0),), epi=log_decay, name="rwkv_w2")
    amid = matmul(xa, [(w["a1"], 0)], n=lora, tm=tm, tn=lora, out_dtypes=(BF16,), name="rwkv_a1")
    a = matmul(amid, [(w["a2"], 0)], n=D, tm=tm, tn=512, extras=((w["a0"], 0),),
               epi=lambda acc, a0: jax.nn.sigmoid(a0 + acc), name="rwkv_a2")
    gl = _mat_shape(w["g1"])[1]
    gmid = matmul(xg, [(w["g1"], 0)], n=gl, tm=tm, tn=gl, epi=jax.nn.sigmoid, out_dtypes=(BF16,), name="rwkv_g1")
    g = matmul(gmid, [(w["g2"], 0)], n=D, tm=tm, tn=512, name="rwkv_g2")
    params = (w["k_k"], w["k_a"], w["r_k"], w["gn_g"], w["gn_b"])
    if L % RWKV_CHUNK == 0:
        y, s_new = rwkv_recurrence(r, k, v, lw, a, g, *params, s0, B=B, L=L, C=RWKV_CHUNK)
    else:
        assert L == 1
        Lp = SAMPLE_ROWS
        pad = lambda t: jnp.pad(t[:B].reshape(B, L, D), ((0, 0), (0, Lp - L), (0, 0))).reshape(B * Lp, D)
        y, s_new = rwkv_recurrence(*[pad(t) for t in (r, k, v, lw, a, g)], *params, s0, B=B, L=Lp, C=Lp)
        y = jnp.pad(y.reshape(B, Lp, D)[:, 0], ((0, M - B), (0, 0)))
    pre1 = matmul(y, [(w["o"], 0)], n=D, tm=tm, tn=512, extras=(x32,),
                  epi=lambda acc, xt: ALPHA * xt + acc, name="rwkv_o")
    return pre1, s_new


def kernel(x_prompt, x_sample, p_prompt, p_sample, cache_moba_k, cache_moba_v, page_table, state_ret, state_wkv, state_shift, ln1_g, ln1_b, ln2_g, ln2_b, w_ffn_in, w_ffn_out, w_ple_gate, w_ple_proj, rel_bias, a_w_qkv, a_w_o, r_w_in, r_gn_g, r_w_o, c_mu, c_w_rkv, c_w0, c_w1, c_w2, c_a0, c_a1, c_a2, c_g1, c_g2, c_k_k, c_k_a, c_r_k, c_gn_g, c_gn_b, c_w_o):
    Bp, Sp, D = x_prompt.shape
    Bs, Ts, _ = x_sample.shape
    assert Ts == 1 and Bs <= SAMPLE_ROWS
    Mp, Ms = Bp * Sp, SAMPLE_ROWS
    past = page_table.shape[1] * PAGE_SIZE
    bf = lambda t: t.astype(BF16)
    pad_rows = lambda t: jnp.pad(t, ((0, Ms - Bs), (0, 0)))

    xp32 = x_prompt.reshape(Mp, D)
    xs32 = pad_rows(x_sample.reshape(Bs, D))
    xp16, xs16 = bf(xp32), bf(xs32)
    k_pool, v_pool = cache_moba_k, cache_moba_v
    gpad = W_GATE_LORA_PAD - c_g1.shape[-1]
    w16 = dict(ffn_in=bf(w_ffn_in), ffn_out=bf(w_ffn_out), ple_gate=bf(w_ple_gate), ple_proj=bf(w_ple_proj),
               qkv=bf(a_w_qkv), a_o=bf(a_w_o), r_in=bf(r_w_in), r_o=bf(r_w_o), rkv=bf(c_w_rkv), w1=bf(c_w1),
               w2=bf(c_w2), a1=bf(c_a1), a2=bf(c_a2), g1=bf(jnp.pad(c_g1, ((0, 0), (0, 0), (0, gpad)))),
               g2=bf(jnp.pad(c_g2, ((0, 0), (0, gpad), (0, 0)))), c_o=bf(c_w_o))

    kp_l, vp_l, ks_l, vs_l, rp_l, rs_l, wp_l, ws_l, hp_l, hs_l = ([] for _ in range(10))
    for i in range(DEPTH):
        kind, j = i % 3, i // 3
        tail = dict(ln1_g=ln1_g[i], ln1_b=ln1_b[i], ln2_g=ln2_g[i], ln2_b=ln2_b[i],
                    ffn_in=(w16["ffn_in"], (i,)), ffn_out=(w16["ffn_out"], (i,)),
                    ple_gate=(w16["ple_gate"], (i,)), ple_proj=(w16["ple_proj"], (i,)))
        if kind == 0:
            w = dict(qkv=(w16["qkv"], (j,)), o=(w16["a_o"], (j,)))
            pre_p, kp, vp = _moba_layer(xp32, xp16, w, rel_bias, M=Mp, prompt=True, B=Bp, S=Sp)
            pre_s, ks, vs = _moba_layer(xs32, xs16, w, rel_bias, M=Ms, prompt=False, B=Bs, k_pool=k_pool,
                                        v_pool=v_pool, layer=j, page_table=page_table)
            kp_l.append(kp.reshape(Bp, Sp, A_HEADS, D // A_HEADS))
            vp_l.append(vp.reshape(Bp, Sp, A_HEADS, D // A_HEADS))
            ks_l.append(ks[:Bs].reshape(Bs, Ts, A_HEADS, D // A_HEADS))
            vs_l.append(vs[:Bs].reshape(Bs, Ts, A_HEADS, D // A_HEADS))
        elif kind == 1:
            w = dict(o=(w16["r_o"], (j,)), gn_g=r_gn_g[j])
            w["in"] = (w16["r_in"], (j,))
            s0 = jnp.zeros((Bp,) + state_ret.shape[2:], state_ret.dtype)
            pre_p, sp = _ret_layer(xp32, xp16, w, s0, 0, M=Mp, B=Bp, L=Sp)
            pre_s, ss = _ret_layer(xs32, xs16, w, state_ret[j], past, M=Ms, B=Bs, L=Ts)
            rp_l.append(sp)
            rs_l.append(ss)
        else:
            w = dict(mu=c_mu[j], w0=c_w0[j].reshape(1, D), a0=c_a0[j].reshape(1, D),
                     k_k=c_k_k[j], k_a=c_k_a[j], r_k=c_r_k[j], gn_g=c_gn_g[j], gn_b=c_gn_b[j], o=(w16["c_o"], (j,)))
            w.update({name: (w16[name], (j,)) for name in ("rkv", "w1", "w2", "a1", "a2", "g1", "g2")})
            xp3 = xp32.reshape(Bp, Sp, D)
            xprev_p = jnp.concatenate([jnp.zeros((Bp, 1, D), F32), xp3[:, :-1]], axis=1).reshape(Mp, D)
            xprev_s = pad_rows(state_shift[j])
            wk0 = jnp.zeros((Bp,) + state_wkv.shape[2:], state_wkv.dtype)
            pre_p, wkp = _rwkv_layer(xp32, xprev_p, w, wk0, M=Mp, B=Bp, L=Sp)
            pre_s, wks = _rwkv_layer(xs32, xprev_s, w, state_wkv[j], M=Ms, B=Bs, L=Ts)
            wp_l.append(wkp)
            ws_l.append(wks)
            hp_l.append(xp3[:, -1])
            hs_l.append(xs32[:Bs])
        xp32, xp16 = _layer_tail(pre_p, bf(p_prompt[i].reshape(Mp, -1)), tail, M=Mp)
        xs32, xs16 = _layer_tail(pre_s, bf(pad_rows(p_sample[i].reshape(Bs, -1))), tail, M=Ms)
    return (xp32.reshape(Bp, Sp, D), xs32[:Bs].reshape(Bs, Ts, D),
            jnp.stack(kp_l), jnp.stack(vp_l), jnp.stack(ks_l), jnp.stack(vs_l),
            jnp.stack(rp_l), jnp.stack(rs_l), jnp.stack(wp_l), jnp.stack(ws_l),
            jnp.stack(hp_l), jnp.stack(hs_l))
```
